```python
import math
import jax, jax.numpy as jnp
from jax import lax
import numpy as np

D_MODEL = 2048
BATCH = 4
SEQ = 4096
DEPTH = 2

HEAD_DIM = 128
CHUNK = 128
QB = 128
N_A_GROUPS = 8
A_WIDTH = N_A_GROUPS * HEAD_DIM
N_B_HEADS = 8
DILATED_CONFIGS = ((128, 1), (512, 4), (2048, 16))
N_B_GROUPS = len(DILATED_CONFIGS)
B_WIDTH = N_B_HEADS * HEAD_DIM
MIX_WIDTH = A_WIDTH + B_WIDTH
IN_WIDTH = 2 * A_WIDTH + N_B_GROUPS * B_WIDTH + 2 * B_WIDTH
D_FF = -(-8 * D_MODEL // (3 * 256)) * 256
ROPE_THETA = 10000.0
EPS = 1e-6

kernel_name = "hybrid_gmlp_dilated_attn_block"


def rms_norm(x, g):
    x32 = x.astype(jnp.float32)
    y = x32 * lax.rsqrt(jnp.mean(x32 * x32, axis=-1, keepdims=True) + EPS)
    return (y * g.astype(jnp.float32)).astype(x.dtype)


def layer_norm(x, g, b):
    x32 = x.astype(jnp.float32)
    mu = jnp.mean(x32, axis=-1, keepdims=True)
    var = jnp.mean(jnp.square(x32 - mu), axis=-1, keepdims=True)
    y = (x32 - mu) * lax.rsqrt(var + EPS)
    return (y * g.astype(jnp.float32) + b.astype(jnp.float32)).astype(x.dtype)


def rope_tables(seq):
    pos = jnp.arange(seq, dtype=jnp.float32)
    inv_freq = 1.0 / (ROPE_THETA ** (jnp.arange(0, HEAD_DIM, 2, dtype=jnp.float32) / HEAD_DIM))
    ang = pos[:, None] * inv_freq[None, :]
    return jnp.cos(ang), jnp.sin(ang)


def apply_rope(x, cos, sin):
    half = HEAD_DIM // 2
    x32 = x.astype(jnp.float32)
    x1, x2 = x32[..., :half], x32[..., half:]
    c = cos[None, :, None, :]
    s = sin[None, :, None, :]
    return jnp.concatenate([x1 * c - x2 * s, x2 * c + x1 * s], axis=-1).astype(x.dtype)


def chunked_spatial_gating(z, ln_g, ln_b, w_s, b_s):
    B, S, _ = z.shape
    z = jax.nn.gelu(z, approximate=False)
    u, v = jnp.split(z, 2, axis=-1)
    v = layer_norm(v.reshape(B, S, N_A_GROUPS, HEAD_DIM), ln_g, ln_b)
    v = v.reshape(B, S // CHUNK, CHUNK, N_A_GROUPS, HEAD_DIM)
    causal = jnp.tril(jnp.ones((CHUNK, CHUNK), dtype=w_s.dtype))
    w = w_s * causal[None]
    gate = jnp.einsum('gij,bnjgc->bnigc', w, v) + b_s.T[None, None, :, :, None]
    return u * gate.reshape(B, S, A_WIDTH).astype(u.dtype)


def _to_residue_blocks(x, r, pad):
    B, S = x.shape[:2]
    rest = x.shape[2:]
    x = jnp.pad(x, ((0, 0), (0, pad)) + ((0, 0),) * len(rest))
    L = (S + pad) // r
    x = jnp.moveaxis(x.reshape((B, L, r) + rest), 2, 1)
    return x.reshape((B, r, L // QB, QB) + rest)


def _from_residue_blocks(x, S):
    B, r, nb = x.shape[:3]
    rest = x.shape[4:]
    x = jnp.moveaxis(x.reshape((B, r, nb * QB) + rest), 1, 2)
    return x.reshape((B, nb * QB * r) + rest)[:, :S]


def dilated_branch(q, k, v, window, dilation):
    B, S, H, D = q.shape
    r = dilation
    w_sub = window // dilation
    pad = (-S) % (r * QB)
    qb = _to_residue_blocks(q, r, pad).astype(jnp.float32)
    kb = _to_residue_blocks(k, r, pad).astype(jnp.float32)
    vb = _to_residue_blocks(v, r, pad).astype(jnp.float32)
    nb = qb.shape[2]
    blk_pad = ((0, 0), (0, 0), (1, 0), (0, 0), (0, 0), (0, 0))
    kk = jnp.concatenate([jnp.pad(kb, blk_pad)[:, :, :-1], kb], axis=3)
    vv = jnp.concatenate([jnp.pad(vb, blk_pad)[:, :, :-1], vb], axis=3)
    s = jnp.einsum('brnqhd,brnkhd->brnhqk', qb, kk) * (D ** -0.5)
    qi = jnp.arange(QB)[:, None]
    kj = jnp.arange(2 * QB)[None, :]
    band = (kj <= QB + qi) & (kj >= QB + qi - w_sub)
    valid = (jnp.arange(nb) > 0)[:, None, None] | (kj >= QB)[None]
    mask = band[None] & valid
    s = jnp.where(mask[None, None, :, None], s, -jnp.inf)
    m = jnp.max(s, axis=-1, keepdims=True)
    p = jnp.exp(s - m)
    l = jnp.sum(p, axis=-1, keepdims=True)
    o = jnp.einsum('brnhqk,brnkhd->brnqhd', p, vv) / jnp.transpose(l, (0, 1, 2, 4, 3, 5))
    lse = jnp.transpose((m + jnp.log(l))[..., 0], (0, 1, 2, 4, 3))
    return _from_residue_blocks(o, S), _from_residue_blocks(lse, S)


def dilated_attention(q, k, v, cos, sin, q_gain, k_gain):
    B, S = q.shape[:2]
    q = q.reshape(B, S, N_B_GROUPS * N_B_HEADS, HEAD_DIM)
    k = k.reshape(B, S, N_B_HEADS, HEAD_DIM)
    v = v.reshape(B, S, N_B_HEADS, HEAD_DIM)
    q = apply_rope(rms_norm(q, q_gain), cos, sin).reshape(B, S, N_B_GROUPS, N_B_HEADS, HEAD_DIM)
    k = apply_rope(rms_norm(k, k_gain), cos, sin)
    outs, lses = [], []
    for gi, (window, dilation) in enumerate(DILATED_CONFIGS):
        o, lse = dilated_branch(q[:, :, gi], k, v, window, dilation)
        outs.append(o)
        lses.append(lse)
    alpha = jax.nn.softmax(jnp.stack(lses, axis=0), axis=0)
    o = jnp.sum(alpha[..., None] * jnp.stack(outs, axis=0), axis=0)
    return o.reshape(B, S, B_WIDTH).astype(v.dtype)


def setup_inputs(seed: int = 0) -> dict:
    key = jax.random.key(seed)
    ks = jax.random.split(key, 17)
    f32 = jnp.float32
    nrm = lambda k, shape, scale: jax.random.normal(k, shape, f32) * scale
    gain = lambda k, shape: 1.0 + 0.02 * jax.random.normal(k, shape, f32)
    return {
        "x": jax.random.normal(ks[0], (BATCH, SEQ, D_MODEL), f32),
        "mix_norm": gain(ks[1], (DEPTH, D_MODEL)),
        "w_in": nrm(ks[2], (DEPTH, D_MODEL, IN_WIDTH), D_MODEL ** -0.5),
        "a_ln_g": gain(ks[3], (DEPTH, N_A_GROUPS, HEAD_DIM)),
        "a_ln_b": nrm(ks[4], (DEPTH, N_A_GROUPS, HEAD_DIM), 0.02),
        "a_w_s": nrm(ks[5], (DEPTH, N_A_GROUPS, CHUNK, CHUNK), CHUNK ** -0.5),
        "a_b_s": gain(ks[6], (DEPTH, N_A_GROUPS, CHUNK)),
        "q_norm": gain(ks[7], (DEPTH, HEAD_DIM)),
        "k_norm": gain(ks[8], (DEPTH, HEAD_DIM)),
        "a_out_norm": gain(ks[9], (DEPTH, A_WIDTH)),
        "b_out_norm": gain(ks[10], (DEPTH, B_WIDTH)),
        "w_out": nrm(ks[11], (DEPTH, MIX_WIDTH, D_MODEL), MIX_WIDTH ** -0.5),
        "ffn_norm": gain(ks[12], (DEPTH, D_MODEL)),
        "w_gate": nrm(ks[13], (DEPTH, D_MODEL, D_FF), D_MODEL ** -0.5),
        "w_up": nrm(ks[14], (DEPTH, D_MODEL, D_FF), D_MODEL ** -0.5),
        "w_down": nrm(ks[15], (DEPTH, D_FF, D_MODEL), D_FF ** -0.5),
    }


def reference(x, mix_norm, w_in, a_ln_g, a_ln_b, a_w_s, a_b_s, q_norm, k_norm,
              a_out_norm, b_out_norm, w_out, ffn_norm, w_gate, w_up, w_down):
    B, S, _ = x.shape
    cos, sin = rope_tables(S)
    c_a = 2 * A_WIDTH
    c_q = c_a + N_B_GROUPS * B_WIDTH
    c_k = c_q + B_WIDTH
    for l in range(DEPTH):
        h = rms_norm(x, mix_norm[l])
        z = h @ w_in[l]
        a_out = chunked_spatial_gating(z[..., :c_a], a_ln_g[l], a_ln_b[l], a_w_s[l], a_b_s[l])
        b_out = dilated_attention(z[..., c_a:c_q], z[..., c_q:c_k], z[..., c_k:],
                                  cos, sin, q_norm[l], k_norm[l])
        mixed = jnp.concatenate([rms_norm(a_out, a_out_norm[l]),
                                 rms_norm(b_out, b_out_norm[l])], axis=-1)
        x = x + mixed @ w_out[l]
        h = rms_norm(x, ffn_norm[l])
        x = x + (jax.nn.silu(h @ w_gate[l]) * (h @ w_up[l])) @ w_down[l]
    return x
```

```python
import functools
import math

import jax
import jax.numpy as jnp
from jax import lax
from jax.experimental import pallas as pl
from jax.experimental.pallas import tpu as pltpu

D_MODEL = 2048
HEAD_DIM = 128
CHUNK = 128
QB = 128
N_A_GROUPS = 8
A_WIDTH = N_A_GROUPS * HEAD_DIM
N_B_HEADS = 8
DILATED_CONFIGS = ((128, 1), (512, 4), (2048, 16))
N_B_GROUPS = len(DILATED_CONFIGS)
B_WIDTH = N_B_HEADS * HEAD_DIM
MIX_WIDTH = A_WIDTH + B_WIDTH
IN_WIDTH = 2 * A_WIDTH + N_B_GROUPS * B_WIDTH + 2 * B_WIDTH
D_FF = -(-8 * D_MODEL // (3 * 256)) * 256
ROPE_THETA = 10000.0
EPS = 1e-6

IN_TILE_N = 1024
IN_N_BLOCKS = IN_WIDTH // IN_TILE_N
IN_GELU_BLOCKS = 2 * A_WIDTH // IN_TILE_N
IN_Q_END = IN_GELU_BLOCKS + N_B_GROUPS
IN_K_BLOCK = IN_Q_END
IN_V_BLOCK = IN_K_BLOCK + 1

TILE_M = 512
GATE_ROWS = 512
FFN_TILE_F = 512
VMEM_LIMIT = 48 * 1024 * 1024

F32 = jnp.float32
BF16 = jnp.bfloat16


def _rms_normalize(x, gain):
    ms = jnp.mean(x * x, axis=-1, keepdims=True)
    return x * lax.rsqrt(ms + EPS) * gain


def _gelu_exact(x):
    return 0.5 * x * (1.0 + lax.erf(x * np_sqrt_half))


np_sqrt_half = float(math.sqrt(0.5))


def _in_proj_kernel(x_ref, g_ref, w_ref, cos_ref, sin_ref, qg_ref, kg_ref,
                    o_ref, h_ref, acc_ref):
    n = pl.program_id(1)

    @pl.when(n == 0)
    def _():
        h_ref[...] = _rms_normalize(x_ref[...], g_ref[...]).astype(BF16)

    acc_ref[...] = jnp.dot(h_ref[...], w_ref[...], preferred_element_type=F32)

    @pl.when(n < IN_GELU_BLOCKS)
    def _():
        o_ref[...] = _gelu_exact(acc_ref[...]).astype(BF16)

    @pl.when((n >= IN_GELU_BLOCKS) & (n < IN_V_BLOCK))
    def _():
        gain = jnp.where(n < IN_Q_END, qg_ref[...] * (HEAD_DIM ** -0.5), kg_ref[...])
        cos = cos_ref[...]
        sin = sin_ref[...]
        for h in range(IN_TILE_N // HEAD_DIM):
            sl = slice(h * HEAD_DIM, (h + 1) * HEAD_DIM)
            y = _rms_normalize(acc_ref[:, sl], gain)
            y = y * cos + pltpu.roll(y, HEAD_DIM // 2, 1) * sin
            o_ref[:, sl] = y.astype(BF16)

    @pl.when(n == IN_V_BLOCK)
    def _():
        o_ref[...] = acc_ref[...].astype(BF16)


def _in_proj(x2, gain, w, cos_full, sin_signed, q_gain, k_gain, seq):
    m = x2.shape[0]
    pos_blocks = seq // TILE_M
    return pl.pallas_call(
        _in_proj_kernel,
        grid=(m // TILE_M, IN_N_BLOCKS),
        in_specs=[
            pl.BlockSpec((TILE_M, D_MODEL), lambda i, n: (i, 0)),
            pl.BlockSpec((1, D_MODEL), lambda i, n: (0, 0)),
            pl.BlockSpec((D_MODEL, IN_TILE_N), lambda i, n: (0, n)),
            pl.BlockSpec((TILE_M, HEAD_DIM), lambda i, n: (i % pos_blocks, 0)),
            pl.BlockSpec((TILE_M, HEAD_DIM), lambda i, n: (i % pos_blocks, 0)),
            pl.BlockSpec((1, HEAD_DIM), lambda i, n: (0, 0)),
            pl.BlockSpec((1, HEAD_DIM), lambda i, n: (0, 0)),
        ],
        out_specs=pl.BlockSpec((TILE_M, IN_TILE_N), lambda i, n: (i, n)),
        out_shape=jax.ShapeDtypeStruct((m, IN_WIDTH), BF16),
        scratch_shapes=[
            pltpu.VMEM((TILE_M, D_MODEL), BF16),
            pltpu.VMEM((TILE_M, IN_TILE_N), F32),
        ],
        compiler_params=pltpu.CompilerParams(
            dimension_semantics=("parallel", "arbitrary"),
            vmem_limit_bytes=VMEM_LIMIT),
        name="in_proj",
    )(x2, gain, w, cos_full, sin_signed, q_gain, k_gain)


def _gating_kernel(z_ref, lng_ref, lnb_ref, ws_ref, bs_ref, og_ref, o_ref, a_ref):
    row = lax.broadcasted_iota(jnp.int32, (CHUNK, CHUNK), 0)
    col = lax.broadcasted_iota(jnp.int32, (CHUNK, CHUNK), 1)
    causal = row >= col
    for c in range(GATE_ROWS // CHUNK):
        rows = slice(c * CHUNK, (c + 1) * CHUNK)
        ss = jnp.zeros((CHUNK, 1), F32)
        for g in range(N_A_GROUPS):
            cols = slice(g * HEAD_DIM, (g + 1) * HEAD_DIM)
            v = z_ref[rows, A_WIDTH + g * HEAD_DIM:A_WIDTH + (g + 1) * HEAD_DIM].astype(F32)
            mu = jnp.mean(v, axis=-1, keepdims=True)
            d = v - mu
            var = jnp.mean(d * d, axis=-1, keepdims=True)
            vn = d * lax.rsqrt(var + EPS) * lng_ref[g:g + 1, :] + lnb_ref[g:g + 1, :]
            w = jnp.where(causal, ws_ref[g], 0.0).astype(BF16)
            gate = jnp.dot(w, vn.astype(BF16), preferred_element_type=F32)
            gate = gate + bs_ref[:, g:g + 1]
            a = z_ref[rows, cols].astype(F32) * gate
            ss = ss + jnp.sum(a * a, axis=-1, keepdims=True)
            a_ref[rows, cols] = a
        inv = lax.rsqrt(ss * (1.0 / A_WIDTH) + EPS)
        o_ref[rows, :] = (a_ref[rows, :] * inv * og_ref[...]).astype(BF16)


def _gating(z, ln_g, ln_b, w_s, b_s_t, out_gain):
    m = z.shape[0]
    return pl.pallas_call(
        _gating_kernel,
        grid=(m // GATE_ROWS,),
        in_specs=[
            pl.BlockSpec((GATE_ROWS, 2 * A_WIDTH), lambda i: (i, 0)),
            pl.BlockSpec((N_A_GROUPS, HEAD_DIM), lambda i: (0, 0)),
            pl.BlockSpec((N_A_GROUPS, HEAD_DIM), lambda i: (0, 0)),
            pl.BlockSpec((N_A_GROUPS, CHUNK, CHUNK), lambda i: (0, 0, 0)),
            pl.BlockSpec((CHUNK, N_A_GROUPS), lambda i: (0, 0)),
            pl.BlockSpec((1, A_WIDTH), lambda i: (0, 0)),
        ],
        out_specs=pl.BlockSpec((GATE_ROWS, A_WIDTH), lambda i: (i, 0)),
        out_shape=jax.ShapeDtypeStruct((m, A_WIDTH), BF16),
        scratch_shapes=[pltpu.VMEM((GATE_ROWS, A_WIDTH), F32)],
        compiler_params=pltpu.CompilerParams(
            dimension_semantics=("parallel",), vmem_limit_bytes=VMEM_LIMIT),
        name="gating",
    )(z, ln_g, ln_b, w_s, b_s_t, out_gain)


def _dilated_kernel(q_ref, kp_ref, kc_ref, vp_ref, vc_ref, o_ref, lse_ref):
    n = pl.program_id(2)
    qi = lax.broadcasted_iota(jnp.int32, (QB, QB), 0)
    kj = lax.broadcasted_iota(jnp.int32, (QB, QB), 1)
    prev_ok = (kj >= qi) & (n > 0)
    cur_ok = kj <= qi
    lane = lax.broadcasted_iota(jnp.int32, (QB, HEAD_DIM), 1)
    lse_all = jnp.zeros((QB, HEAD_DIM), F32)
    dn = (((1,), (1,)), ((), ()))
    for h in range(N_B_HEADS):
        sl = slice(h * HEAD_DIM, (h + 1) * HEAD_DIM)
        q = q_ref[:, sl]
        sp = lax.dot_general(q, kp_ref[:, sl], dn, preferred_element_type=F32)
        sc = lax.dot_general(q, kc_ref[:, sl], dn, preferred_element_type=F32)
        sp = jnp.where(prev_ok, sp, -jnp.inf)
        sc = jnp.where(cur_ok, sc, -jnp.inf)
        m = jnp.maximum(jnp.max(sp, axis=-1, keepdims=True),
                        jnp.max(sc, axis=-1, keepdims=True))
        pp = jnp.exp(sp - m)
        pc = jnp.exp(sc - m)
        l = jnp.sum(pp, axis=-1, keepdims=True) + jnp.sum(pc, axis=-1, keepdims=True)
        o = jnp.dot(pp.astype(BF16), vp_ref[:, sl], preferred_element_type=F32)
        o = o + jnp.dot(pc.astype(BF16), vc_ref[:, sl], preferred_element_type=F32)
        o_ref[:, sl] = (o / l).astype(BF16)
        lse_all = jnp.where(lane == h, m + jnp.log(l), lse_all)
    lse_ref[...] = lse_all


def _dilated_group(z3, gi, dilation, batch, seq):
    r = dilation
    length = seq // r
    nb = length // QB
    blocks_per_row = IN_WIDTH // B_WIDTH
    zv = z3.reshape(batch, length, r * IN_WIDTH)
    q_blk = IN_GELU_BLOCKS + gi

    def col(c, blk):
        return c * blocks_per_row + blk

    o, lse = pl.pallas_call(
        _dilated_kernel,
        grid=(batch, r, nb),
        in_specs=[
            pl.BlockSpec((None, QB, B_WIDTH), lambda b, c, n: (b, n, col(c, q_blk))),
            pl.BlockSpec((None, QB, B_WIDTH),
                         lambda b, c, n: (b, jnp.maximum(n - 1, 0), col(c, IN_K_BLOCK))),
            pl.BlockSpec((None, QB, B_WIDTH), lambda b, c, n: (b, n, col(c, IN_K_BLOCK))),
            pl.BlockSpec((None, QB, B_WIDTH),
                         lambda b, c, n: (b, jnp.maximum(n - 1, 0), col(c, IN_V_BLOCK))),
            pl.BlockSpec((None, QB, B_WIDTH), lambda b, c, n: (b, n, col(c, IN_V_BLOCK))),
        ],
        out_specs=[
            pl.BlockSpec((None, QB, B_WIDTH), lambda b, c, n: (b, n, c)),
            pl.BlockSpec((None, QB, HEAD_DIM), lambda b, c, n: (b, n, c)),
        ],
        out_shape=[
            jax.ShapeDtypeStruct((batch, length, r * B_WIDTH), BF16),
            jax.ShapeDtypeStruct((batch, length, r * HEAD_DIM), F32),
        ],
        compiler_params=pltpu.CompilerParams(
            dimension_semantics=("parallel", "parallel", "arbitrary"),
            vmem_limit_bytes=VMEM_LIMIT),
        name=f"dilated_r{r}",
    )(zv, zv, zv, zv, zv)
    return (o.reshape(batch * seq, B_WIDTH), lse.reshape(batch * seq, HEAD_DIM))


def _out_proj_kernel(x_ref, a_ref, o0_ref, o1_ref, o2_ref, l0_ref, l1_ref, l2_ref,
                     bg_ref, w_ref, out_ref, mix_ref, b_ref):
    l0, l1, l2 = l0_ref[...], l1_ref[...], l2_ref[...]
    mx = jnp.maximum(jnp.maximum(l0, l1), l2)
    e0, e1, e2 = jnp.exp(l0 - mx), jnp.exp(l1 - mx), jnp.exp(l2 - mx)
    den = e0 + e1 + e2
    al0, al1, al2 = e0 / den, e1 / den, e2 / den
    ss = jnp.zeros((TILE_M, 1), F32)
    for h in range(N_B_HEADS):
        sl = slice(h * HEAD_DIM, (h + 1) * HEAD_DIM)
        b = (al0[:, h:h + 1] * o0_ref[:, sl].astype(F32)
             + al1[:, h:h + 1] * o1_ref[:, sl].astype(F32)
             + al2[:, h:h + 1] * o2_ref[:, sl].astype(F32))
        ss = ss + jnp.sum(b * b, axis=-1, keepdims=True)
        b_ref[:, sl] = b
    inv = lax.rsqrt(ss * (1.0 / B_WIDTH) + EPS)
    mix_ref[:, :A_WIDTH] = a_ref[...]
    mix_ref[:, A_WIDTH:] = (b_ref[...] * inv * bg_ref[...]).astype(BF16)
    out_ref[...] = x_ref[...] + jnp.dot(mix_ref[...], w_ref[...],
                                        preferred_element_type=F32)


def _out_proj(x2, a_n, os_, lses, b_gain, w):
    m = x2.shape[0]
    row = lambda i: (i, 0)
    fixed = lambda i: (0, 0)
    return pl.pallas_call(
        _out_proj_kernel,
        grid=(m // TILE_M,),
        in_specs=[
            pl.BlockSpec((TILE_M, D_MODEL), row),
            pl.BlockSpec((TILE_M, A_WIDTH), row),
            pl.BlockSpec((TILE_M, B_WIDTH), row),
            pl.BlockSpec((TILE_M, B_WIDTH), row),
            pl.BlockSpec((TILE_M, B_WIDTH), row),
            pl.BlockSpec((TILE_M, HEAD_DIM), row),
            pl.BlockSpec((TILE_M, HEAD_DIM), row),
            pl.BlockSpec((TILE_M, HEAD_DIM), row),
            pl.BlockSpec((1, B_WIDTH), fixed),
            pl.BlockSpec((MIX_WIDTH, D_MODEL), fixed),
        ],
        out_specs=pl.BlockSpec((TILE_M, D_MODEL), row),
        out_shape=jax.ShapeDtypeStruct((m, D_MODEL), F32),
        scratch_shapes=[
            pltpu.VMEM((TILE_M, MIX_WIDTH), BF16),
            pltpu.VMEM((TILE_M, B_WIDTH), F32),
        ],
        compiler_params=pltpu.CompilerParams(
            dimension_semantics=("parallel",), vmem_limit_bytes=VMEM_LIMIT),
        name="out_proj",
    )(x2, a_n, *os_, *lses, b_gain, w)


def _ffn_kernel(x_ref, g_ref, wg_ref, wu_ref, wd_ref, out_ref, h_ref):
    f = pl.program_id(1)

    @pl.when(f == 0)
    def _():
        x = x_ref[...]
        h_ref[...] = _rms_normalize(x, g_ref[...]).astype(BF16)
        out_ref[...] = x

    h = h_ref[...]
    gate = jnp.dot(h, wg_ref[...], preferred_element_type=F32)
    up = jnp.dot(h, wu_ref[...], preferred_element_type=F32)
    act = (gate * jax.nn.sigmoid(gate) * up).astype(BF16)
    out_ref[...] += jnp.dot(act, wd_ref[...], preferred_element_type=F32)


def _ffn(x2, gain, w_gate, w_up, w_down):
    m = x2.shape[0]
    return pl.pallas_call(
        _ffn_kernel,
        grid=(m // TILE_M, D_FF // FFN_TILE_F),
        in_specs=[
            pl.BlockSpec((TILE_M, D_MODEL), lambda i, f: (i, 0)),
            pl.BlockSpec((1, D_MODEL), lambda i, f: (0, 0)),
            pl.BlockSpec((D_MODEL, FFN_TILE_F), lambda i, f: (0, f)),
            pl.BlockSpec((D_MODEL, FFN_TILE_F), lambda i, f: (0, f)),
            pl.BlockSpec((FFN_TILE_F, D_MODEL), lambda i, f: (f, 0)),
        ],
        out_specs=pl.BlockSpec((TILE_M, D_MODEL), lambda i, f: (i, 0)),
        out_shape=jax.ShapeDtypeStruct((m, D_MODEL), F32),
        scratch_shapes=[pltpu.VMEM((TILE_M, D_MODEL), BF16)],
        compiler_params=pltpu.CompilerParams(
            dimension_semantics=("parallel", "arbitrary"),
            vmem_limit_bytes=VMEM_LIMIT),
        name="ffn",
    )(x2, gain, w_gate, w_up, w_down)


def _rope_tables(seq):
    pos = jnp.arange(seq, dtype=F32)
    inv_freq = 1.0 / (ROPE_THETA ** (jnp.arange(0, HEAD_DIM, 2, dtype=F32) / HEAD_DIM))
    ang = pos[:, None] * inv_freq[None, :]
    cos, sin = jnp.cos(ang), jnp.sin(ang)
    return (jnp.concatenate([cos, cos], axis=-1),
            jnp.concatenate([-sin, sin], axis=-1))


def kernel(x, mix_norm, w_in, a_ln_g, a_ln_b, a_w_s, a_b_s, q_norm, k_norm,
           a_out_norm, b_out_norm, w_out, ffn_norm, w_gate, w_up, w_down):
    batch, seq, d_model = x.shape
    depth = w_in.shape[0]
    assert d_model == D_MODEL and seq % TILE_M == 0
    assert all(seq % (r * QB) == 0 for _, r in DILATED_CONFIGS)
    assert all(w == r * QB for w, r in DILATED_CONFIGS)
    cos_full, sin_signed = _rope_tables(seq)
    x2 = x.reshape(batch * seq, d_model)
    for l in range(depth):
        z = _in_proj(x2, mix_norm[l][None], w_in[l].astype(BF16), cos_full, sin_signed,
                     q_norm[l][None], k_norm[l][None], seq)
        a_n = _gating(z, a_ln_g[l], a_ln_b[l], a_w_s[l], a_b_s[l].T, a_out_norm[l][None])
        z3 = z.reshape(batch, seq, IN_WIDTH)
        os_, lses = [], []
        for gi, (_, dilation) in enumerate(DILATED_CONFIGS):
            o, lse = _dilated_group(z3, gi, dilation, batch, seq)
            os_.append(o)
            lses.append(lse)
        x2 = _out_proj(x2, a_n, os_, lses, b_out_norm[l][None], w_out[l].astype(BF16))
        x2 = _ffn(x2, ffn_norm[l][None], w_gate[l].astype(BF16),
                  w_up[l].astype(BF16), w_down[l].astype(BF16))
    return x2.reshape(batch, seq, d_model)
```

```python
import functools
import math

import jax
import jax.numpy as jnp
from jax import lax
from jax.experimental import pallas as pl
from jax.experimental.pallas import tpu as pltpu

D_MODEL = 2048
HEAD_DIM = 128
CHUNK = 128
QB = 128
N_A_GROUPS = 8
A_WIDTH = N_A_GROUPS * HEAD_DIM
N_B_HEADS = 8
DILATED_CONFIGS = ((128, 1), (512, 4), (2048, 16))
DILATIONS = tuple(r for _, r in DILATED_CONFIGS)
N_B_GROUPS = len(DILATED_CONFIGS)
B_WIDTH = N_B_HEADS * HEAD_DIM
MIX_WIDTH = A_WIDTH + B_WIDTH
IN_WIDTH = 2 * A_WIDTH + N_B_GROUPS * B_WIDTH + 2 * B_WIDTH
D_FF = -(-8 * D_MODEL // (3 * 256)) * 256
ROPE_THETA = 10000.0
EPS = 1e-6

IN_TILE_N = 1024
IN_N_BLOCKS = IN_WIDTH // IN_TILE_N
IN_GELU_BLOCKS = 2 * A_WIDTH // IN_TILE_N
IN_Q_BLOCK0 = IN_GELU_BLOCKS
IN_K_BLOCK = IN_Q_BLOCK0 + N_B_GROUPS
IN_V_BLOCK = IN_K_BLOCK + 1
SUB_N = 256

TILE_M = 512
GATE_ROWS = 512
FFN_TILE_F = 512
ATT_ROWS = 2 * QB
ATT_LOOKAHEAD = 2
VMEM_LIMIT = 48 * 1024 * 1024

F32 = jnp.float32
BF16 = jnp.bfloat16
SQRT_HALF = float(math.sqrt(0.5))
LOG2_E = float(math.log2(math.e))
LN_2 = float(math.log(2.0))
NEG_INF = float("-inf")


def _rms_normalize(x, gain):
    ms = jnp.mean(x * x, axis=-1, keepdims=True)
    return x * lax.rsqrt(ms + EPS) * gain


def _gelu_exact(x):
    return 0.5 * x * (1.0 + lax.erf(x * SQRT_HALF))


def _head(h):
    return slice(h * HEAD_DIM, (h + 1) * HEAD_DIM)


def _store_residue_layouts(y, h, dsts, slab_ref):
    if any(r > 1 for _, r in dsts):
        slab_ref[h] = y
    for dst, r in dsts:
        if r == 1:
            dst[:, _head(h)] = y.astype(BF16)
            continue
        rows = TILE_M // r
        for c in range(r):
            piece = slab_ref[h, pl.ds(c, rows, stride=r), :]
            dst[c * rows:(c + 1) * rows, _head(h)] = piece.astype(BF16)


def _in_proj_kernel(x_ref, g_ref, w_ref, cos_ref, sin_ref, qg_ref, kg_ref,
                    uv_ref, q_ref, k_ref, v_ref, h_ref, slab_ref):
    n = pl.program_id(1)

    @pl.when(n == 0)
    def _():
        h_ref[...] = _rms_normalize(x_ref[...], g_ref[...]).astype(BF16)

    def sub_dot(j):
        return jnp.dot(h_ref[...], w_ref[:, j * SUB_N:(j + 1) * SUB_N],
                       preferred_element_type=F32)

    n_sub = IN_TILE_N // SUB_N
    heads_per_sub = SUB_N // HEAD_DIM

    @pl.when(n < IN_GELU_BLOCKS)
    def _():
        for j in range(n_sub):
            uv_ref[:, j * SUB_N:(j + 1) * SUB_N] = _gelu_exact(sub_dot(j)).astype(BF16)

    def qk_branch(gain, dsts):
        cos = cos_ref[...]
        sin = sin_ref[...]
        for j in range(n_sub):
            acc = sub_dot(j)
            for hh in range(heads_per_sub):
                y = _rms_normalize(acc[:, _head(hh)], gain)
                y = y * cos + pltpu.roll(y, HEAD_DIM // 2, 1) * sin
                _store_residue_layouts(y, j * heads_per_sub + hh, dsts, slab_ref)

    q_gain = qg_ref[...] * (HEAD_DIM ** -0.5 * LOG2_E)
    for gi, r in enumerate(DILATIONS):
        pl.when(n == IN_Q_BLOCK0 + gi)(
            lambda r=r: qk_branch(q_gain, [(q_ref, r)]))

    pl.when(n == IN_K_BLOCK)(
        lambda: qk_branch(kg_ref[...], [(k_ref.at[gi], r) for gi, r in enumerate(DILATIONS)]))

    @pl.when(n == IN_V_BLOCK)
    def _():
        dsts = [(v_ref.at[gi], r) for gi, r in enumerate(DILATIONS)]
        for j in range(n_sub):
            acc = sub_dot(j)
            for hh in range(heads_per_sub):
                _store_residue_layouts(acc[:, _head(hh)], j * heads_per_sub + hh, dsts, slab_ref)


def _in_proj(x2, gain, w, cos_full, sin_signed, q_gain, k_gain, seq):
    m = x2.shape[0]
    pos_blocks = seq // TILE_M
    q_slot = lambda n: jnp.clip(n - IN_Q_BLOCK0, 0, N_B_GROUPS - 1)
    return pl.pallas_call(
        _in_proj_kernel,
        grid=(m // TILE_M, IN_N_BLOCKS),
        in_specs=[
            pl.BlockSpec((TILE_M, D_MODEL), lambda i, n: (i, 0)),
            pl.BlockSpec((1, D_MODEL), lambda i, n: (0, 0)),
            pl.BlockSpec((D_MODEL, IN_TILE_N), lambda i, n: (0, n)),
            pl.BlockSpec((TILE_M, HEAD_DIM), lambda i, n: (i % pos_blocks, 0)),
            pl.BlockSpec((TILE_M, HEAD_DIM), lambda i, n: (i % pos_blocks, 0)),
            pl.BlockSpec((1, HEAD_DIM), lambda i, n: (0, 0)),
            pl.BlockSpec((1, HEAD_DIM), lambda i, n: (0, 0)),
        ],
        out_specs=[
            pl.BlockSpec((TILE_M, IN_TILE_N),
                         lambda i, n: (i, jnp.minimum(n, IN_GELU_BLOCKS - 1))),
            pl.BlockSpec((None, TILE_M, B_WIDTH), lambda i, n: (q_slot(n), i, 0)),
            pl.BlockSpec((N_B_GROUPS, TILE_M, B_WIDTH), lambda i, n: (0, i, 0)),
            pl.BlockSpec((N_B_GROUPS, TILE_M, B_WIDTH), lambda i, n: (0, i, 0)),
        ],
        out_shape=[
            jax.ShapeDtypeStruct((m, 2 * A_WIDTH), BF16),
            jax.ShapeDtypeStruct((N_B_GROUPS, m, B_WIDTH), BF16),
            jax.ShapeDtypeStruct((N_B_GROUPS, m, B_WIDTH), BF16),
            jax.ShapeDtypeStruct((N_B_GROUPS, m, B_WIDTH), BF16),
        ],
        scratch_shapes=[
            pltpu.VMEM((TILE_M, D_MODEL), BF16),
            pltpu.VMEM((N_B_HEADS, TILE_M, HEAD_DIM), F32),
        ],
        compiler_params=pltpu.CompilerParams(
            dimension_semantics=("parallel", "arbitrary"),
            vmem_limit_bytes=VMEM_LIMIT),
        name="in_proj",
    )(x2, gain, w, cos_full, sin_signed, q_gain, k_gain)


def _gating_kernel(z_ref, lng_ref, lnb_ref, ws_ref, bs_ref, og_ref, o_ref,
                   wm_ref, vn_ref, a_ref):
    row = lax.broadcasted_iota(jnp.int32, (CHUNK, CHUNK), 0)
    col = lax.broadcasted_iota(jnp.int32, (CHUNK, CHUNK), 1)
    for g in range(N_A_GROUPS):
        wm_ref[g] = jnp.where(row >= col, ws_ref[g], 0.0).astype(BF16)

    def chunk_rows(c):
        return slice(c * CHUNK, (c + 1) * CHUNK)

    def layer_norm(c):
        for g in range(N_A_GROUPS):
            v = z_ref[chunk_rows(c), A_WIDTH + g * HEAD_DIM:A_WIDTH + (g + 1) * HEAD_DIM]
            v = v.astype(F32)
            mu = jnp.mean(v, axis=-1, keepdims=True)
            d = v - mu
            var = jnp.mean(d * d, axis=-1, keepdims=True)
            vn = d * lax.rsqrt(var + EPS) * lng_ref[g:g + 1, :] + lnb_ref[g:g + 1, :]
            vn_ref[chunk_rows(c), _head(g)] = vn.astype(BF16)

    def mix(c):
        return [jnp.dot(wm_ref[g], vn_ref[chunk_rows(c), _head(g)],
                        preferred_element_type=F32) for g in range(N_A_GROUPS)]

    def gate(c, mixed):
        sq = jnp.zeros((CHUNK, HEAD_DIM), F32)
        for g in range(N_A_GROUPS):
            a = z_ref[chunk_rows(c), _head(g)].astype(F32) * (mixed[g] + bs_ref[:, g:g + 1])
            sq = sq + a * a
            a_ref[chunk_rows(c), _head(g)] = a
        ms = jnp.sum(sq, axis=-1, keepdims=True) * (1.0 / A_WIDTH)
        o_ref[chunk_rows(c), :] = (a_ref[chunk_rows(c), :] * lax.rsqrt(ms + EPS)
                                   * og_ref[...]).astype(BF16)

    n_chunks = GATE_ROWS // CHUNK
    layer_norm(0)
    for c in range(n_chunks):
        mixed = mix(c)
        if c + 1 < n_chunks:
            layer_norm(c + 1)
        gate(c, mixed)


def _gating(uv, ln_g, ln_b, w_s, b_s_t, out_gain):
    m = uv.shape[0]
    return pl.pallas_call(
        _gating_kernel,
        grid=(m // GATE_ROWS,),
        in_specs=[
            pl.BlockSpec((GATE_ROWS, 2 * A_WIDTH), lambda i: (i, 0)),
            pl.BlockSpec((N_A_GROUPS, HEAD_DIM), lambda i: (0, 0)),
            pl.BlockSpec((N_A_GROUPS, HEAD_DIM), lambda i: (0, 0)),
            pl.BlockSpec((N_A_GROUPS, CHUNK, CHUNK), lambda i: (0, 0, 0)),
            pl.BlockSpec((CHUNK, N_A_GROUPS), lambda i: (0, 0)),
            pl.BlockSpec((1, A_WIDTH), lambda i: (0, 0)),
        ],
        out_specs=pl.BlockSpec((GATE_ROWS, A_WIDTH), lambda i: (i, 0)),
        out_shape=jax.ShapeDtypeStruct((m, A_WIDTH), BF16),
        scratch_shapes=[
            pltpu.VMEM((N_A_GROUPS, CHUNK, CHUNK), BF16),
            pltpu.VMEM((GATE_ROWS, A_WIDTH), BF16),
            pltpu.VMEM((GATE_ROWS, A_WIDTH), F32),
        ],
        compiler_params=pltpu.CompilerParams(
            dimension_semantics=("parallel",), vmem_limit_bytes=VMEM_LIMIT),
        name="gating",
    )(uv, ln_g, ln_b, w_s, b_s_t, out_gain)


def _dilated_kernel(q_ref, k_ref, v_ref, o_ref, lse_ref, kk_ref, ve_ref, *, piece_rows):
    s = pl.program_id(2)
    n_pieces = ATT_ROWS // piece_rows
    win = QB + ATT_ROWS

    @pl.when(s == 0)
    def _():
        kk_ref[0:QB, :] = jnp.zeros((QB, B_WIDTH), BF16)
        ve_ref[:, 0:QB, 0:HEAD_DIM] = jnp.zeros((N_B_HEADS, QB, HEAD_DIM), BF16)
        ve_ref[:, :, HEAD_DIM:] = jnp.ones((N_B_HEADS, win, HEAD_DIM), BF16)

    @pl.when(s > 0)
    def _():
        kk_ref[0:QB, :] = kk_ref[ATT_ROWS:win, :]
        ve_ref[:, 0:QB, 0:HEAD_DIM] = ve_ref[:, ATT_ROWS:win, 0:HEAD_DIM]

    for p in range(n_pieces):
        rows = slice(QB + p * piece_rows, QB + (p + 1) * piece_rows)
        kk_ref[rows, :] = k_ref[p]
        for h in range(N_B_HEADS):
            ve_ref[h, rows, 0:HEAD_DIM] = v_ref[p, :, _head(h)]

    qi = lax.broadcasted_iota(jnp.int32, (QB, 2 * QB), 0)
    kj = lax.broadcasted_iota(jnp.int32, (QB, 2 * QB), 1)
    band = (kj >= qi) & (kj <= qi + QB)
    bias_band = jnp.where(band, 0.0, NEG_INF)
    bias_first = jnp.where(band & (kj >= QB), 0.0, NEG_INF)
    lane = lax.broadcasted_iota(jnp.int32, (QB, HEAD_DIM), 1)
    dn = (((1,), (1,)), ((), ()))

    def block_rows(ref, j, cols):
        if piece_rows >= QB:
            per = piece_rows // QB
            return ref[j // per, (j % per) * QB:(j % per + 1) * QB, cols]
        per = QB // piece_rows
        return jnp.concatenate([ref[j * per + t, :, cols] for t in range(per)], axis=0)

    def store_rows(ref, j, cols, val):
        if piece_rows >= QB:
            per = piece_rows // QB
            ref[j // per, (j % per) * QB:(j % per + 1) * QB, cols] = val
            return
        per = QB // piece_rows
        for t in range(per):
            ref[j * per + t, :, cols] = val[t * piece_rows:(t + 1) * piece_rows]

    units = [(j, h) for j in range(ATT_ROWS // QB) for h in range(N_B_HEADS)]

    def scores(u):
        j, h = units[u]
        q = block_rows(q_ref, j, _head(h))
        sc = lax.dot_general(q, kk_ref[j * QB:j * QB + 2 * QB, _head(h)], dn,
                             preferred_element_type=F32)
        bias = jnp.where(s == 0, bias_first, bias_band) if j == 0 else bias_band
        return sc + bias

    pending = {u: scores(u) for u in range(ATT_LOOKAHEAD)}
    lse_tiles = [jnp.zeros((QB, HEAD_DIM), F32) for _ in range(ATT_ROWS // QB)]
    for u, (j, h) in enumerate(units):
        sc = pending.pop(u)
        m = jnp.max(sc, axis=-1, keepdims=True)
        p = jnp.exp2(sc - m).astype(BF16)
        if u + ATT_LOOKAHEAD < len(units):
            pending[u + ATT_LOOKAHEAD] = scores(u + ATT_LOOKAHEAD)
        oe = jnp.dot(p, ve_ref[h, j * QB:j * QB + 2 * QB, :], preferred_element_type=F32)
        den = oe[:, HEAD_DIM:]
        store_rows(o_ref, j, _head(h), (oe[:, :HEAD_DIM] / den).astype(BF16))
        lse = m * LN_2 + jnp.log(den[:, 0:1])
        lse_tiles[j] = jnp.where(lane == h, lse, lse_tiles[j])
    for j, tile in enumerate(lse_tiles):
        store_rows(lse_ref, j, slice(None), tile)


def _dilated_group(q_all, k_all, v_all, gi, dilation, batch, seq):
    r = dilation
    piece_rows = min(TILE_M // r, ATT_ROWS)
    n_pieces = ATT_ROWS // piece_rows
    tiles = seq // (r * piece_rows)
    steps = tiles // n_pieces
    view = lambda a: a.reshape(N_B_GROUPS, batch, tiles, r, piece_rows, B_WIDTH)
    in_spec = pl.BlockSpec((None, None, n_pieces, None, piece_rows, B_WIDTH),
                           lambda b, c, s: (gi, b, s, c, 0, 0))
    o, lse = pl.pallas_call(
        functools.partial(_dilated_kernel, piece_rows=piece_rows),
        grid=(batch, r, steps),
        in_specs=[in_spec, in_spec, in_spec],
        out_specs=[
            pl.BlockSpec((None, n_pieces, None, piece_rows, B_WIDTH),
                         lambda b, c, s: (b, s, c, 0, 0)),
            pl.BlockSpec((None, n_pieces, None, piece_rows, HEAD_DIM),
                         lambda b, c, s: (b, s, c, 0, 0)),
        ],
        out_shape=[
            jax.ShapeDtypeStruct((batch, tiles, r, piece_rows, B_WIDTH), BF16),
            jax.ShapeDtypeStruct((batch, tiles, r, piece_rows, HEAD_DIM), F32),
        ],
        scratch_shapes=[
            pltpu.VMEM((QB + ATT_ROWS, B_WIDTH), BF16),
            pltpu.VMEM((N_B_HEADS, QB + ATT_ROWS, 2 * HEAD_DIM), BF16),
        ],
        compiler_params=pltpu.CompilerParams(
            dimension_semantics=("parallel", "parallel", "arbitrary"),
            vmem_limit_bytes=VMEM_LIMIT),
        name=f"dilated_r{r}",
    )(view(q_all), view(k_all), view(v_all))
    return (o.reshape(batch * seq, B_WIDTH), lse.reshape(batch * seq, HEAD_DIM))


def _out_proj_kernel(x_ref, a_ref, o0_ref, o1_ref, o2_ref, l0_ref, l1_ref, l2_ref,
                     bg_ref, w_ref, out_ref, mix_ref, b_ref, lslab_ref, oslab_ref):
    o_refs = (o0_ref, o1_ref, o2_ref)
    l_refs = (l0_ref, l1_ref, l2_ref)

    def to_position_order(dst, src, r, cols):
        rows = TILE_M // r
        for c in range(r):
            dst[pl.ds(c, rows, stride=r), :] = src[c * rows:(c + 1) * rows, cols].astype(F32)

    lses = []
    for gi, r in enumerate(DILATIONS):
        if r == 1:
            lses.append(l_refs[gi][...])
        else:
            to_position_order(lslab_ref.at[gi], l_refs[gi], r, slice(None))
            lses.append(lslab_ref[gi])
    mx = jnp.maximum(jnp.maximum(lses[0], lses[1]), lses[2])
    es = [jnp.exp(l - mx) for l in lses]
    den = es[0] + es[1] + es[2]
    head_of_lane = lax.broadcasted_iota(jnp.int32, (HEAD_DIM, B_WIDTH), 1) // HEAD_DIM
    src_lane = lax.broadcasted_iota(jnp.int32, (HEAD_DIM, B_WIDTH), 0)
    spread = jnp.where(head_of_lane == src_lane, 1.0, 0.0).astype(BF16)
    alphas = []
    for e in es:
        al = e / den
        hi = al.astype(BF16)
        lo = (al - hi.astype(F32)).astype(BF16)
        alphas.append(jnp.dot(hi, spread, preferred_element_type=F32)
                      + jnp.dot(lo, spread, preferred_element_type=F32))

    sq = jnp.zeros((TILE_M, HEAD_DIM), F32)
    for h in range(N_B_HEADS):
        b = None
        for gi, r in enumerate(DILATIONS):
            if r == 1:
                o = o_refs[gi][:, _head(h)].astype(F32)
            else:
                to_position_order(oslab_ref.at[gi, h], o_refs[gi], r, _head(h))
                o = oslab_ref[gi, h]
            term = alphas[gi][:, _head(h)] * o
            b = term if b is None else b + term
        sq = sq + b * b
        b_ref[:, _head(h)] = b
    ms = jnp.sum(sq, axis=-1, keepdims=True) * (1.0 / B_WIDTH)
    mix_ref[:, :A_WIDTH] = a_ref[...]
    mix_ref[:, A_WIDTH:] = (b_ref[...] * lax.rsqrt(ms + EPS) * bg_ref[...]).astype(BF16)
    out_ref[...] = x_ref[...] + jnp.dot(mix_ref[...], w_ref[...],
                                        preferred_element_type=F32)


def _out_proj(x2, a_n, os_, lses, b_gain, w):
    m = x2.shape[0]
    row = lambda i: (i, 0)
    fixed = lambda i: (0, 0)
    return pl.pallas_call(
        _out_proj_kernel,
        grid=(m // TILE_M,),
        in_specs=[
            pl.BlockSpec((TILE_M, D_MODEL), row),
            pl.BlockSpec((TILE_M, A_WIDTH), row),
            pl.BlockSpec((TILE_M, B_WIDTH), row),
            pl.BlockSpec((TILE_M, B_WIDTH), row),
            pl.BlockSpec((TILE_M, B_WIDTH), row),
            pl.BlockSpec((TILE_M, HEAD_DIM), row),
            pl.BlockSpec((TILE_M, HEAD_DIM), row),
            pl.BlockSpec((TILE_M, HEAD_DIM), row),
            pl.BlockSpec((1, B_WIDTH), fixed),
            pl.BlockSpec((MIX_WIDTH, D_MODEL), fixed),
        ],
        out_specs=pl.BlockSpec((TILE_M, D_MODEL), row),
        out_shape=jax.ShapeDtypeStruct((m, D_MODEL), F32),
        scratch_shapes=[
            pltpu.VMEM((TILE_M, MIX_WIDTH), BF16),
            pltpu.VMEM((TILE_M, B_WIDTH), F32),
            pltpu.VMEM((N_B_GROUPS, TILE_M, HEAD_DIM), F32),
            pltpu.VMEM((N_B_GROUPS, N_B_HEADS, TILE_M, HEAD_DIM), F32),
        ],
        compiler_params=pltpu.CompilerParams(
            dimension_semantics=("parallel",), vmem_limit_bytes=VMEM_LIMIT),
        name="out_proj",
    )(x2, a_n, *os_, *lses, b_gain, w)


def _ffn_kernel(x_ref, g_ref, wg_ref, wu_ref, wd_ref, out_ref, h_ref):
    f = pl.program_id(1)

    @pl.when(f == 0)
    def _():
        x = x_ref[...]
        h_ref[...] = _rms_normalize(x, g_ref[...]).astype(BF16)
        out_ref[...] = x

    h = h_ref[...]
    gate = jnp.dot(h, wg_ref[...], preferred_element_type=F32)
    up = jnp.dot(h, wu_ref[...], preferred_element_type=F32)
    act = (gate * jax.nn.sigmoid(gate) * up).astype(BF16)
    out_ref[...] += jnp.dot(act, wd_ref[...], preferred_element_type=F32)


def _ffn(x2, gain, w_gate, w_up, w_down):
    m = x2.shape[0]
    return pl.pallas_call(
        _ffn_kernel,
        grid=(m // TILE_M, D_FF // FFN_TILE_F),
        in_specs=[
            pl.BlockSpec((TILE_M, D_MODEL), lambda i, f: (i, 0)),
            pl.BlockSpec((1, D_MODEL), lambda i, f: (0, 0)),
            pl.BlockSpec((D_MODEL, FFN_TILE_F), lambda i, f: (0, f)),
            pl.BlockSpec((D_MODEL, FFN_TILE_F), lambda i, f: (0, f)),
            pl.BlockSpec((FFN_TILE_F, D_MODEL), lambda i, f: (f, 0)),
        ],
        out_specs=pl.BlockSpec((TILE_M, D_MODEL), lambda i, f: (i, 0)),
        out_shape=jax.ShapeDtypeStruct((m, D_MODEL), F32),
        scratch_shapes=[pltpu.VMEM((TILE_M, D_MODEL), BF16)],
        compiler_params=pltpu.CompilerParams(
            dimension_semantics=("parallel", "arbitrary"),
            vmem_limit_bytes=VMEM_LIMIT),
        name="ffn",
    )(x2, gain, w_gate, w_up, w_down)


def _rope_tables(seq):
    pos = jnp.arange(seq, dtype=F32)
    inv_freq = 1.0 / (ROPE_THETA ** (jnp.arange(0, HEAD_DIM, 2, dtype=F32) / HEAD_DIM))
    ang = pos[:, None] * inv_freq[None, :]
    cos, sin = jnp.cos(ang), jnp.sin(ang)
    return (jnp.concatenate([cos, cos], axis=-1),
            jnp.concatenate([-sin, sin], axis=-1))


def kernel(x, mix_norm, w_in, a_ln_g, a_ln_b, a_w_s, a_b_s, q_norm, k_norm,
           a_out_norm, b_out_norm, w_out, ffn_norm, w_gate, w_up, w_down):
    batch, seq, d_model = x.shape
    depth = w_in.shape[0]
    assert d_model == D_MODEL and seq % TILE_M == 0
    assert all(w == r * QB for w, r in DILATED_CONFIGS)
    assert all(TILE_M % (r * 16) == 0 and seq % (r * ATT_ROWS) == 0 for r in DILATIONS)
    cos_full, sin_signed = _rope_tables(seq)
    x2 = x.reshape(batch * seq, d_model)
    for l in range(depth):
        uv, q_all, k_all, v_all = _in_proj(
            x2, mix_norm[l][None], w_in[l].astype(BF16), cos_full, sin_signed,
            q_norm[l][None], k_norm[l][None], seq)
        a_n = _gating(uv, a_ln_g[l], a_ln_b[l], a_w_s[l], a_b_s[l].T, a_out_norm[l][None])
        os_, lses = [], []
        for gi, r in enumerate(DILATIONS):
            o, lse = _dilated_group(q_all, k_all, v_all, gi, r, batch, seq)
            os_.append(o)
            lses.append(lse)
        x2 = _out_proj(x2, a_n, os_, lses, b_out_norm[l][None], w_out[l].astype(BF16))
        x2 = _ffn(x2, ffn_norm[l][None], w_gate[l].astype(BF16),
                  w_up[l].astype(BF16), w_down[l].astype(BF16))
    return x2.reshape(batch, seq, d_model)
```

```python
import functools
import math

import jax
import jax.numpy as jnp
from jax import lax
from jax.experimental import pallas as pl
from jax.experimental.pallas import tpu as pltpu

D_MODEL = 2048
HEAD_DIM = 128
CHUNK = 128
QB = 128
N_A_GROUPS = 8
A_WIDTH = N_A_GROUPS * HEAD_DIM
N_B_HEADS = 8
DILATED_CONFIGS = ((128, 1), (512, 4), (2048, 16))
DILATIONS = tuple(r for _, r in DILATED_CONFIGS)
N_B_GROUPS = len(DILATED_CONFIGS)
B_WIDTH = N_B_HEADS * HEAD_DIM
MIX_WIDTH = A_WIDTH + B_WIDTH
IN_WIDTH = 2 * A_WIDTH + N_B_GROUPS * B_WIDTH + 2 * B_WIDTH
D_FF = -(-8 * D_MODEL // (3 * 256)) * 256
ROPE_THETA = 10000.0
EPS = 1e-6

IN_TILE_N = 1024
IN_N_BLOCKS = IN_WIDTH // IN_TILE_N
IN_GELU_BLOCKS = 2 * A_WIDTH // IN_TILE_N
IN_Q_BLOCK0 = IN_GELU_BLOCKS
IN_K_BLOCK = IN_Q_BLOCK0 + N_B_GROUPS
IN_V_BLOCK = IN_K_BLOCK + 1
SUB_N = 256
HEADS_PER_SUB = SUB_N // HEAD_DIM

TILE_M = 512
EPI_ROWS = 64
EPI_SLOTS = 4
GATE_ROWS = 1024
FFN_TILE_F = 512
ATT_ROWS = 2 * QB
ATT_UNITS = (ATT_ROWS // QB) * N_B_HEADS
VMEM_LIMIT = 48 * 1024 * 1024

F32 = jnp.float32
BF16 = jnp.bfloat16
SQRT_HALF = float(math.sqrt(0.5))
LOG2_E = float(math.log2(math.e))
LN_2 = float(math.log(2.0))
NEG_INF = float("-inf")


def _rms_normalize(x, gain):
    ms = jnp.mean(x * x, axis=-1, keepdims=True)
    return x * lax.rsqrt(ms + EPS) * gain


def _gelu_exact(x):
    return 0.5 * x * (1.0 + lax.erf(x * SQRT_HALF))


def _head(h):
    return slice(h * HEAD_DIM, (h + 1) * HEAD_DIM)


def _in_proj_kernel(x_ref, g_ref, w_ref, cos_ref, sin_ref, qg_ref, kg_ref,
                    uv_ref, q_ref, k_ref, v_ref, h_ref, acc_ref, s_ref, t_ref,
                    inv_ref, y_ref, r_ref):
    t = pl.program_id(0)
    n = t % IN_N_BLOCKS

    def normalize_rows():
        h_ref[...] = _rms_normalize(x_ref[...], g_ref[...]).astype(BF16)

    def multiply(j):
        res = jnp.dot(h_ref[...], w_ref[:, j * SUB_N:(j + 1) * SUB_N],
                      preferred_element_type=F32)
        for hh in range(HEADS_PER_SUB):
            acc_ref[j * HEADS_PER_SUB + hh] = res[:, _head(hh)]

    def chunks():
        return [slice(c, c + EPI_ROWS) for c in range(0, TILE_M, EPI_ROWS)]

    def gelu_epilogue(h):
        for rows in chunks():
            uv_ref[rows, _head(h)] = _gelu_exact(acc_ref[h, rows, :]).astype(BF16)

    def split4(src, h, dst_f32, dst_bf16):
        rows = TILE_M // 4
        for c in range(4):
            piece = src[h, pl.ds(c, rows, stride=4), :]
            if dst_f32 is not None:
                dst_f32[h, c * rows:(c + 1) * rows, :] = piece
            if dst_bf16 is not None:
                dst_bf16[c * rows:(c + 1) * rows, _head(h)] = piece.astype(BF16)

    def split16_from4(h, dst_bf16):
        rows4 = TILE_M // 4
        rows16 = TILE_M // 16
        for c_lo in range(4):
            for c_hi in range(4):
                c = 4 * c_hi + c_lo
                piece = t_ref[h, pl.ds(c_lo * rows4 + c_hi, rows16, stride=4), :]
                dst_bf16[c * rows16:(c + 1) * rows16, _head(h)] = piece.astype(BF16)

    def residue_layouts(src, h, dsts):
        need16 = 16 in dsts
        if 4 in dsts or need16:
            split4(src, h, t_ref if need16 else None, dsts.get(4))
        if need16:
            split16_from4(h, dsts[16])

    def qk_epilogue(gain, dsts, h):
        strided = any(r > 1 for r in dsts)
        w = h % EPI_SLOTS
        for rows in chunks():
            a = acc_ref[h, rows, :]
            ms = jnp.mean(a * a, axis=-1, keepdims=True)
            inv_ref[w, rows, :] = jnp.broadcast_to(lax.rsqrt(ms + EPS), (EPI_ROWS, HEAD_DIM))
        for rows in chunks():
            y_ref[w, rows, :] = acc_ref[h, rows, :] * inv_ref[w, rows, :] * gain
        for rows in chunks():
            r_ref[w, rows, :] = pltpu.roll(y_ref[w, rows, :], HEAD_DIM // 2, 1)
        for rows in chunks():
            y = y_ref[w, rows, :] * cos_ref[rows, :] + r_ref[w, rows, :] * sin_ref[rows, :]
            if 1 in dsts:
                dsts[1][rows, _head(h)] = y.astype(BF16)
            if strided:
                s_ref[h, rows, :] = y
        residue_layouts(s_ref, h, dsts)

    def v_epilogue(h):
        dsts = {r: v_ref.at[gi] for gi, r in enumerate(DILATIONS)}
        for rows in chunks():
            dsts[1][rows, _head(h)] = acc_ref[h, rows, :].astype(BF16)
        residue_layouts(acc_ref, h, dsts)

    q_gain = qg_ref[...] * (HEAD_DIM ** -0.5 * LOG2_E)
    epilogues = {}
    for b in range(IN_GELU_BLOCKS):
        epilogues[b] = gelu_epilogue
    for gi, r in enumerate(DILATIONS):
        epilogues[IN_Q_BLOCK0 + gi] = functools.partial(qk_epilogue, q_gain, {r: q_ref})
    epilogues[IN_K_BLOCK] = functools.partial(
        qk_epilogue, kg_ref[...], {r: k_ref.at[gi] for gi, r in enumerate(DILATIONS)})
    epilogues[IN_V_BLOCK] = v_epilogue

    @pl.when(t == 0)
    def _():
        normalize_rows()
        for j in range(IN_TILE_N // SUB_N):
            multiply(j)

    for b in range(IN_N_BLOCKS):
        @pl.when((n == b) & (t > 0))
        def _(b=b):
            for j in range(IN_TILE_N // SUB_N):
                for hh in range(HEADS_PER_SUB):
                    epilogues[(b - 1) % IN_N_BLOCKS](j * HEADS_PER_SUB + hh)
                multiply(j)
            if b == IN_N_BLOCKS - 1:
                normalize_rows()


def _in_proj(x2, gain, w_all, layer, cos_full, sin_signed, q_gain, k_gain, seq):
    m = x2.shape[0]
    row_tiles = m // TILE_M
    pos_blocks = seq // TILE_M
    nb = IN_N_BLOCKS
    steps = row_tiles * nb + 1

    mul_tile = lambda t: jnp.minimum((t + 1) // nb, row_tiles - 1)
    epi_tile = lambda t: jnp.maximum(t - 1, 0) // nb
    epi_blk = lambda t: jnp.maximum(t - 1, 0) % nb
    q_slot = lambda t: jnp.clip(epi_blk(t) - IN_Q_BLOCK0, 0, N_B_GROUPS - 1)
    return pl.pallas_call(
        _in_proj_kernel,
        grid=(steps,),
        in_specs=[
            pl.BlockSpec((TILE_M, D_MODEL), lambda t: (mul_tile(t), 0)),
            pl.BlockSpec((None, 1, D_MODEL), lambda t: (layer, 0, 0)),
            pl.BlockSpec((None, D_MODEL, IN_TILE_N), lambda t: (layer, 0, t % nb)),
            pl.BlockSpec((TILE_M, HEAD_DIM), lambda t: (epi_tile(t) % pos_blocks, 0)),
            pl.BlockSpec((TILE_M, HEAD_DIM), lambda t: (epi_tile(t) % pos_blocks, 0)),
            pl.BlockSpec((None, 1, HEAD_DIM), lambda t: (layer, 0, 0)),
            pl.BlockSpec((None, 1, HEAD_DIM), lambda t: (layer, 0, 0)),
        ],
        out_specs=[
            pl.BlockSpec((TILE_M, IN_TILE_N),
                         lambda t: (epi_tile(t), jnp.minimum(epi_blk(t), IN_GELU_BLOCKS - 1))),
            pl.BlockSpec((None, TILE_M, B_WIDTH), lambda t: (q_slot(t), epi_tile(t), 0)),
            pl.BlockSpec((N_B_GROUPS, TILE_M, B_WIDTH), lambda t: (0, epi_tile(t), 0)),
            pl.BlockSpec((N_B_GROUPS, TILE_M, B_WIDTH), lambda t: (0, epi_tile(t), 0)),
        ],
        out_shape=[
            jax.ShapeDtypeStruct((m, 2 * A_WIDTH), BF16),
            jax.ShapeDtypeStruct((N_B_GROUPS, m, B_WIDTH), BF16),
            jax.ShapeDtypeStruct((N_B_GROUPS, m, B_WIDTH), BF16),
            jax.ShapeDtypeStruct((N_B_GROUPS, m, B_WIDTH), BF16),
        ],
        scratch_shapes=[
            pltpu.VMEM((TILE_M, D_MODEL), BF16),
            pltpu.VMEM((N_B_HEADS, TILE_M, HEAD_DIM), F32),
            pltpu.VMEM((N_B_HEADS, TILE_M, HEAD_DIM), F32),
            pltpu.VMEM((N_B_HEADS, TILE_M, HEAD_DIM), F32),
            pltpu.VMEM((EPI_SLOTS, TILE_M, HEAD_DIM), F32),
            pltpu.VMEM((EPI_SLOTS, TILE_M, HEAD_DIM), F32),
            pltpu.VMEM((EPI_SLOTS, TILE_M, HEAD_DIM), F32),
        ],
        compiler_params=pltpu.CompilerParams(
            dimension_semantics=("arbitrary",),
            vmem_limit_bytes=VMEM_LIMIT),
        name="in_proj",
    )(x2, gain, w_all, cos_full, sin_signed, q_gain, k_gain)


def _gating_kernel(z_ref, lng_ref, lnb_ref, ws_ref, bs_ref, og_ref, o_ref,
                   wm_ref, vn_ref, a_ref, mu_ref, is_ref, mx_ref, rs_ref):
    row = lax.broadcasted_iota(jnp.int32, (CHUNK, CHUNK), 0)
    col = lax.broadcasted_iota(jnp.int32, (CHUNK, CHUNK), 1)
    for g in range(N_A_GROUPS):
        wm_ref[g] = jnp.where(row >= col, ws_ref[g], 0.0).astype(BF16)

    def chunk_rows(c):
        return slice(c * CHUNK, (c + 1) * CHUNK)

    def v_tile(c, g):
        return z_ref[chunk_rows(c), A_WIDTH + g * HEAD_DIM:A_WIDTH + (g + 1) * HEAD_DIM].astype(F32)

    tile = (CHUNK, HEAD_DIM)

    def ln_mean(c):
        for g in range(N_A_GROUPS):
            mu = jnp.mean(v_tile(c, g), axis=-1, keepdims=True)
            mu_ref[c % 2, g] = jnp.broadcast_to(mu, tile)

    def ln_var(c):
        for g in range(N_A_GROUPS):
            d = v_tile(c, g) - mu_ref[c % 2, g]
            var = jnp.mean(d * d, axis=-1, keepdims=True)
            is_ref[c % 2, g] = jnp.broadcast_to(lax.rsqrt(var + EPS), tile)

    def ln_apply(c):
        for g in range(N_A_GROUPS):
            vn = ((v_tile(c, g) - mu_ref[c % 2, g]) * is_ref[c % 2, g]
                  * lng_ref[g:g + 1, :] + lnb_ref[g:g + 1, :])
            vn_ref[chunk_rows(c), _head(g)] = vn.astype(BF16)

    def mix(c):
        for g in range(N_A_GROUPS):
            mx_ref[c % 2, g] = jnp.dot(wm_ref[g], vn_ref[chunk_rows(c), _head(g)],
                                       preferred_element_type=F32)

    def gate(c):
        sq = jnp.zeros(tile, F32)
        for g in range(N_A_GROUPS):
            a = z_ref[chunk_rows(c), _head(g)].astype(F32) * (mx_ref[c % 2, g] + bs_ref[g])
            sq = sq + a * a
            a_ref[chunk_rows(c), _head(g)] = a
        ms = jnp.sum(sq, axis=-1, keepdims=True) * (1.0 / A_WIDTH)
        rs_ref[c % 2] = jnp.broadcast_to(lax.rsqrt(ms + EPS), tile)

    def finish(c):
        for g in range(N_A_GROUPS):
            o_ref[chunk_rows(c), _head(g)] = (
                a_ref[chunk_rows(c), _head(g)] * rs_ref[c % 2] * og_ref[:, _head(g)]
            ).astype(BF16)

    stages = (ln_mean, ln_var, ln_apply, mix, gate, finish)
    n_chunks = GATE_ROWS // CHUNK
    for it in range(n_chunks + len(stages) - 1):
        for k in reversed(range(len(stages))):
            c = it - k
            if 0 <= c < n_chunks:
                stages[k](c)


def _gating(uv, ln_g, ln_b, w_s, b_s_b, out_gain):
    m = uv.shape[0]
    return pl.pallas_call(
        _gating_kernel,
        grid=(m // GATE_ROWS,),
        in_specs=[
            pl.BlockSpec((GATE_ROWS, 2 * A_WIDTH), lambda i: (i, 0)),
            pl.BlockSpec((N_A_GROUPS, HEAD_DIM), lambda i: (0, 0)),
            pl.BlockSpec((N_A_GROUPS, HEAD_DIM), lambda i: (0, 0)),
            pl.BlockSpec((N_A_GROUPS, CHUNK, CHUNK), lambda i: (0, 0, 0)),
            pl.BlockSpec((N_A_GROUPS, CHUNK, HEAD_DIM), lambda i: (0, 0, 0)),
            pl.BlockSpec((1, A_WIDTH), lambda i: (0, 0)),
        ],
        out_specs=pl.BlockSpec((GATE_ROWS, A_WIDTH), lambda i: (i, 0)),
        out_shape=jax.ShapeDtypeStruct((m, A_WIDTH), BF16),
        scratch_shapes=[
            pltpu.VMEM((N_A_GROUPS, CHUNK, CHUNK), BF16),
            pltpu.VMEM((GATE_ROWS, A_WIDTH), BF16),
            pltpu.VMEM((GATE_ROWS, A_WIDTH), F32),
            pltpu.VMEM((2, N_A_GROUPS, CHUNK, HEAD_DIM), F32),
            pltpu.VMEM((2, N_A_GROUPS, CHUNK, HEAD_DIM), F32),
            pltpu.VMEM((2, N_A_GROUPS, CHUNK, HEAD_DIM), F32),
            pltpu.VMEM((2, CHUNK, HEAD_DIM), F32),
        ],
        compiler_params=pltpu.CompilerParams(
            dimension_semantics=("parallel",), vmem_limit_bytes=VMEM_LIMIT),
        name="gating",
    )(uv, ln_g, ln_b, w_s, b_s_b, out_gain)


def _dilated_kernel(q_ref, k_ref, v_ref, o_ref, lse_ref, kk_ref, ve_ref,
                    sc_ref, m_ref, p_ref, *, piece_rows):
    s = pl.program_id(2)
    n_pieces = ATT_ROWS // piece_rows
    win = QB + ATT_ROWS

    @pl.when(s == 0)
    def _():
        kk_ref[0:QB, :] = jnp.zeros((QB, B_WIDTH), BF16)
        ve_ref[:, 0:QB, 0:HEAD_DIM] = jnp.zeros((N_B_HEADS, QB, HEAD_DIM), BF16)
        ve_ref[:, :, HEAD_DIM:] = jnp.ones((N_B_HEADS, win, HEAD_DIM), BF16)

    @pl.when(s > 0)
    def _():
        kk_ref[0:QB, :] = kk_ref[ATT_ROWS:win, :]
        ve_ref[:, 0:QB, 0:HEAD_DIM] = ve_ref[:, ATT_ROWS:win, 0:HEAD_DIM]

    for p in range(n_pieces):
        rows = slice(QB + p * piece_rows, QB + (p + 1) * piece_rows)
        kk_ref[rows, :] = k_ref[p]
        for h in range(N_B_HEADS):
            ve_ref[h, rows, 0:HEAD_DIM] = v_ref[p, :, _head(h)]

    qi = lax.broadcasted_iota(jnp.int32, (QB, 2 * QB), 0)
    kj = lax.broadcasted_iota(jnp.int32, (QB, 2 * QB), 1)
    band = (kj >= qi) & (kj <= qi + QB)
    bias_band = jnp.where(band, 0.0, NEG_INF)
    bias_first = jnp.where(band & (kj >= QB), 0.0, NEG_INF)
    lane = lax.broadcasted_iota(jnp.int32, (QB, HEAD_DIM), 1)
    dn = (((1,), (1,)), ((), ()))

    def block_rows(ref, j, cols):
        if piece_rows >= QB:
            per = piece_rows // QB
            return ref[j // per, (j % per) * QB:(j % per + 1) * QB, cols]
        per = QB // piece_rows
        return jnp.concatenate([ref[j * per + t, :, cols] for t in range(per)], axis=0)

    def store_rows(ref, j, cols, val):
        if piece_rows >= QB:
            per = piece_rows // QB
            ref[j // per, (j % per) * QB:(j % per + 1) * QB, cols] = val
            return
        per = QB // piece_rows
        for t in range(per):
            ref[j * per + t, :, cols] = val[t * piece_rows:(t + 1) * piece_rows]

    units = [(j, h) for j in range(ATT_ROWS // QB) for h in range(N_B_HEADS)]

    def scores(u):
        j, h = units[u]
        q = block_rows(q_ref, j, _head(h))
        sc = lax.dot_general(q, kk_ref[j * QB:j * QB + 2 * QB, _head(h)], dn,
                             preferred_element_type=F32)
        bias = jnp.where(s == 0, bias_first, bias_band) if j == 0 else bias_band
        sc_ref[u] = sc + bias

    def row_max(u):
        m = jnp.max(sc_ref[u], axis=-1, keepdims=True)
        m_ref[u] = jnp.broadcast_to(m, (QB, HEAD_DIM))

    def probs(u):
        m = m_ref[u]
        for half in range(2):
            p_ref[u, :, _head(half)] = jnp.exp2(sc_ref[u, :, _head(half)] - m).astype(BF16)

    def weighted_values(u):
        j, h = units[u]
        oe = jnp.dot(p_ref[u], ve_ref[h, j * QB:j * QB + 2 * QB, :],
                     preferred_element_type=F32)
        den = oe[:, HEAD_DIM:]
        store_rows(o_ref, j, _head(h), (oe[:, :HEAD_DIM] / den).astype(BF16))
        m_ref[u] = m_ref[u] * LN_2 + jnp.log(den)

    stages = (scores, row_max, probs, weighted_values)
    for it in range(-(len(stages) - 1), len(units)):
        for depth, stage in enumerate(stages):
            u = it + len(stages) - 1 - depth
            if 0 <= u < len(units):
                stage(u)

    for j in range(ATT_ROWS // QB):
        tile = jnp.zeros((QB, HEAD_DIM), F32)
        for h in range(N_B_HEADS):
            tile = jnp.where(lane == h, m_ref[j * N_B_HEADS + h], tile)
        store_rows(lse_ref, j, slice(None), tile)


def _dilated_group(q_all, k_all, v_all, gi, dilation, batch, seq):
    r = dilation
    piece_rows = min(TILE_M // r, ATT_ROWS)
    n_pieces = ATT_ROWS // piece_rows
    tiles = seq // (r * piece_rows)
    steps = tiles // n_pieces
    view = lambda a: a.reshape(N_B_GROUPS, batch, tiles, r, piece_rows, B_WIDTH)
    in_spec = pl.BlockSpec((None, None, n_pieces, None, piece_rows, B_WIDTH),
                           lambda b, c, s: (gi, b, s, c, 0, 0))
    o, lse = pl.pallas_call(
        functools.partial(_dilated_kernel, piece_rows=piece_rows),
        grid=(batch, r, steps),
        in_specs=[in_spec, in_spec, in_spec],
        out_specs=[
            pl.BlockSpec((None, n_pieces, None, piece_rows, B_WIDTH),
                         lambda b, c, s: (b, s, c, 0, 0)),
            pl.BlockSpec((None, n_pieces, None, piece_rows, HEAD_DIM),
                         lambda b, c, s: (b, s, c, 0, 0)),
        ],
        out_shape=[
            jax.ShapeDtypeStruct((batch, tiles, r, piece_rows, B_WIDTH), BF16),
            jax.ShapeDtypeStruct((batch, tiles, r, piece_rows, HEAD_DIM), F32),
        ],
        scratch_shapes=[
            pltpu.VMEM((QB + ATT_ROWS, B_WIDTH), BF16),
            pltpu.VMEM((N_B_HEADS, QB + ATT_ROWS, 2 * HEAD_DIM), BF16),
            pltpu.VMEM((ATT_UNITS, QB, 2 * QB), F32),
            pltpu.VMEM((ATT_UNITS, QB, HEAD_DIM), F32),
            pltpu.VMEM((ATT_UNITS, QB, 2 * QB), BF16),
        ],
        compiler_params=pltpu.CompilerParams(
            dimension_semantics=("parallel", "parallel", "arbitrary"),
            vmem_limit_bytes=VMEM_LIMIT),
        name=f"dilated_r{r}",
    )(view(q_all), view(k_all), view(v_all))
    return (o.reshape(batch * seq, B_WIDTH), lse.reshape(batch * seq, HEAD_DIM))


def _out_proj_kernel(x_ref, a_ref, o0_ref, o1_ref, o2_ref, l0_ref, l1_ref, l2_ref,
                     bg_ref, w_ref, out_ref, mix_ref, b_ref, lslab_ref, oslab_ref):
    o_refs = (o0_ref, o1_ref, o2_ref)
    l_refs = (l0_ref, l1_ref, l2_ref)

    def to_position_order(dst, src, r, cols):
        rows = TILE_M // r
        for c in range(r):
            dst[pl.ds(c, rows, stride=r), :] = src[c * rows:(c + 1) * rows, cols].astype(F32)

    lses = []
    for gi, r in enumerate(DILATIONS):
        if r == 1:
            lses.append(l_refs[gi][...])
        else:
            to_position_order(lslab_ref.at[gi], l_refs[gi], r, slice(None))
            lses.append(lslab_ref[gi])
    mx = jnp.maximum(jnp.maximum(lses[0], lses[1]), lses[2])
    es = [jnp.exp(l - mx) for l in lses]
    den = es[0] + es[1] + es[2]
    head_of_lane = lax.broadcasted_iota(jnp.int32, (HEAD_DIM, B_WIDTH), 1) // HEAD_DIM
    src_lane = lax.broadcasted_iota(jnp.int32, (HEAD_DIM, B_WIDTH), 0)
    spread = jnp.where(head_of_lane == src_lane, 1.0, 0.0).astype(BF16)
    alphas = []
    for e in es:
        al = e / den
        hi = al.astype(BF16)
        lo = (al - hi.astype(F32)).astype(BF16)
        alphas.append(jnp.dot(hi, spread, preferred_element_type=F32)
                      + jnp.dot(lo, spread, preferred_element_type=F32))

    sq = jnp.zeros((TILE_M, HEAD_DIM), F32)
    for h in range(N_B_HEADS):
        b = None
        for gi, r in enumerate(DILATIONS):
            if r == 1:
                o = o_refs[gi][:, _head(h)].astype(F32)
            else:
                to_position_order(oslab_ref.at[gi, h], o_refs[gi], r, _head(h))
                o = oslab_ref[gi, h]
            term = alphas[gi][:, _head(h)] * o
            b = term if b is None else b + term
        sq = sq + b * b
        b_ref[:, _head(h)] = b
    ms = jnp.sum(sq, axis=-1, keepdims=True) * (1.0 / B_WIDTH)
    mix_ref[:, :A_WIDTH] = a_ref[...]
    mix_ref[:, A_WIDTH:] = (b_ref[...] * lax.rsqrt(ms + EPS) * bg_ref[...]).astype(BF16)
    out_ref[...] = x_ref[...] + jnp.dot(mix_ref[...], w_ref[...],
                                        preferred_element_type=F32)


def _out_proj(x2, a_n, os_, lses, b_gain, w_all, layer):
    m = x2.shape[0]
    row = lambda i: (i, 0)
    fixed = lambda i: (layer, 0, 0)
    return pl.pallas_call(
        _out_proj_kernel,
        grid=(m // TILE_M,),
        in_specs=[
            pl.BlockSpec((TILE_M, D_MODEL), row),
            pl.BlockSpec((TILE_M, A_WIDTH), row),
            pl.BlockSpec((TILE_M, B_WIDTH), row),
            pl.BlockSpec((TILE_M, B_WIDTH), row),
            pl.BlockSpec((TILE_M, B_WIDTH), row),
            pl.BlockSpec((TILE_M, HEAD_DIM), row),
            pl.BlockSpec((TILE_M, HEAD_DIM), row),
            pl.BlockSpec((TILE_M, HEAD_DIM), row),
            pl.BlockSpec((None, 1, B_WIDTH), fixed),
            pl.BlockSpec((None, MIX_WIDTH, D_MODEL), fixed),
        ],
        out_specs=pl.BlockSpec((TILE_M, D_MODEL), row),
        out_shape=jax.ShapeDtypeStruct((m, D_MODEL), F32),
        scratch_shapes=[
            pltpu.VMEM((TILE_M, MIX_WIDTH), BF16),
            pltpu.VMEM((TILE_M, B_WIDTH), F32),
            pltpu.VMEM((N_B_GROUPS, TILE_M, HEAD_DIM), F32),
            pltpu.VMEM((N_B_GROUPS, N_B_HEADS, TILE_M, HEAD_DIM), F32),
        ],
        compiler_params=pltpu.CompilerParams(
            dimension_semantics=("parallel",), vmem_limit_bytes=VMEM_LIMIT),
        name="out_proj",
    )(x2, a_n, *os_, *lses, b_gain, w_all)


def _ffn_kernel(x_ref, g_ref, wg_ref, wu_ref, wd_ref, out_ref, h_ref):
    f = pl.program_id(1)

    @pl.when(f == 0)
    def _():
        x = x_ref[...]
        h_ref[...] = _rms_normalize(x, g_ref[...]).astype(BF16)
        out_ref[...] = x

    h = h_ref[...]
    gate = jnp.dot(h, wg_ref[...], preferred_element_type=F32)
    up = jnp.dot(h, wu_ref[...], preferred_element_type=F32)
    act = (gate * jax.nn.sigmoid(gate) * up).astype(BF16)
    out_ref[...] += jnp.dot(act, wd_ref[...], preferred_element_type=F32)


def _ffn(x2, gain, w_gate, w_up, w_down, layer):
    m = x2.shape[0]
    return pl.pallas_call(
        _ffn_kernel,
        grid=(m // TILE_M, D_FF // FFN_TILE_F),
        in_specs=[
            pl.BlockSpec((TILE_M, D_MODEL), lambda i, f: (i, 0)),
            pl.BlockSpec((None, 1, D_MODEL), lambda i, f: (layer, 0, 0)),
            pl.BlockSpec((None, D_MODEL, FFN_TILE_F), lambda i, f: (layer, 0, f)),
            pl.BlockSpec((None, D_MODEL, FFN_TILE_F), lambda i, f: (layer, 0, f)),
            pl.BlockSpec((None, FFN_TILE_F, D_MODEL), lambda i, f: (layer, f, 0)),
        ],
        out_specs=pl.BlockSpec((TILE_M, D_MODEL), lambda i, f: (i, 0)),
        out_shape=jax.ShapeDtypeStruct((m, D_MODEL), F32),
        scratch_shapes=[pltpu.VMEM((TILE_M, D_MODEL), BF16)],
        compiler_params=pltpu.CompilerParams(
            dimension_semantics=("parallel", "arbitrary"),
            vmem_limit_bytes=VMEM_LIMIT),
        name="ffn",
    )(x2, gain, w_gate, w_up, w_down)


def _rope_tables(seq):
    pos = jnp.arange(seq, dtype=F32)
    inv_freq = 1.0 / (ROPE_THETA ** (jnp.arange(0, HEAD_DIM, 2, dtype=F32) / HEAD_DIM))
    ang = pos[:, None] * inv_freq[None, :]
    cos, sin = jnp.cos(ang), jnp.sin(ang)
    return (jnp.concatenate([cos, cos], axis=-1),
            jnp.concatenate([-sin, sin], axis=-1))


def kernel(x, mix_norm, w_in, a_ln_g, a_ln_b, a_w_s, a_b_s, q_norm, k_norm,
           a_out_norm, b_out_norm, w_out, ffn_norm, w_gate, w_up, w_down):
    batch, seq, d_model = x.shape
    depth = w_in.shape[0]
    assert d_model == D_MODEL and seq % TILE_M == 0
    assert all(w == r * QB for w, r in DILATED_CONFIGS)
    assert all(TILE_M % (r * 16) == 0 and seq % (r * ATT_ROWS) == 0 for r in DILATIONS)
    cos_full, sin_signed = _rope_tables(seq)
    x2 = x.reshape(batch * seq, d_model)
    w_in, w_out, w_gate, w_up, w_down = (
        w.astype(BF16) for w in (w_in, w_out, w_gate, w_up, w_down))
    as_rows = lambda p: p.reshape(depth, 1, p.shape[-1])
    mix_norm, q_norm, k_norm, b_out_norm, ffn_norm = (
        as_rows(p) for p in (mix_norm, q_norm, k_norm, b_out_norm, ffn_norm))
    for l in range(depth):
        uv, q_all, k_all, v_all = _in_proj(
            x2, mix_norm, w_in, l, cos_full, sin_signed, q_norm, k_norm, seq)
        b_s_b = jnp.broadcast_to(a_b_s[l][:, :, None], (N_A_GROUPS, CHUNK, HEAD_DIM))
        a_n = _gating(uv, a_ln_g[l], a_ln_b[l], a_w_s[l], b_s_b, a_out_norm[l][None])
        os_, lses = [], []
        for gi, r in enumerate(DILATIONS):
            o, lse = _dilated_group(q_all, k_all, v_all, gi, r, batch, seq)
            os_.append(o)
            lses.append(lse)
        x2 = _out_proj(x2, a_n, os_, lses, b_out_norm, w_out, l)
        x2 = _ffn(x2, ffn_norm, w_gate, w_up, w_down, l)
    return x2.reshape(batch, seq, d_model)
```

```python
import functools
import math

import jax
import jax.numpy as jnp
from jax import lax
from jax.experimental import pallas as pl
from jax.experimental.pallas import tpu as pltpu

D_MODEL = 2048
HEAD_DIM = 128
CHUNK = 128
QB = 128
N_A_GROUPS = 8
A_WIDTH = N_A_GROUPS * HEAD_DIM
N_B_HEADS = 8
DILATED_CONFIGS = ((128, 1), (512, 4), (2048, 16))
DILATIONS = tuple(r for _, r in DILATED_CONFIGS)
N_B_GROUPS = len(DILATED_CONFIGS)
B_WIDTH = N_B_HEADS * HEAD_DIM
MIX_WIDTH = A_WIDTH + B_WIDTH
IN_WIDTH = 2 * A_WIDTH + N_B_GROUPS * B_WIDTH + 2 * B_WIDTH
D_FF = -(-8 * D_MODEL // (3 * 256)) * 256
ROPE_THETA = 10000.0
EPS = 1e-6

IN_TILE_N = 1024
IN_N_BLOCKS = IN_WIDTH // IN_TILE_N
IN_GELU_BLOCKS = 2 * A_WIDTH // IN_TILE_N
IN_Q_BLOCK0 = IN_GELU_BLOCKS
IN_K_BLOCK = IN_Q_BLOCK0 + N_B_GROUPS
IN_V_BLOCK = IN_K_BLOCK + 1
SUB_N = 256
HEADS_PER_SUB = SUB_N // HEAD_DIM

TILE_M = 512
EPI_ROWS = 64
EPI_SLOTS = 4
GATE_ROWS = 1024
FFN_TILE_F = 512
MAX_ATT_ROWS = 4 * QB
OUT_SUB_N = 256
OUT_SLOTS = 2
VMEM_LIMIT = 52 * 1024 * 1024

F32 = jnp.float32
BF16 = jnp.bfloat16
SQRT_HALF = float(math.sqrt(0.5))
LOG2_E = float(math.log2(math.e))
LN_2 = float(math.log(2.0))
NEG_INF = float("-inf")


def _rms_normalize(x, gain):
    ms = jnp.mean(x * x, axis=-1, keepdims=True)
    return x * lax.rsqrt(ms + EPS) * gain


def _gelu_exact(x):
    return 0.5 * x * (1.0 + lax.erf(x * SQRT_HALF))


def _head(h):
    return slice(h * HEAD_DIM, (h + 1) * HEAD_DIM)


def _in_proj_kernel(x_ref, g_ref, w_ref, cos_ref, sin_ref, qg_ref, kg_ref,
                    uv_ref, q_ref, k_ref, v_ref, h_ref, acc_ref, s_ref, t_ref,
                    inv_ref, y_ref, r_ref):
    t = pl.program_id(0)
    n = t % IN_N_BLOCKS

    def normalize_rows():
        h_ref[...] = _rms_normalize(x_ref[...], g_ref[...]).astype(BF16)

    def multiply(j):
        res = jnp.dot(h_ref[...], w_ref[:, j * SUB_N:(j + 1) * SUB_N],
                      preferred_element_type=F32)
        for hh in range(HEADS_PER_SUB):
            acc_ref[j * HEADS_PER_SUB + hh] = res[:, _head(hh)]

    def chunks():
        return [slice(c, c + EPI_ROWS) for c in range(0, TILE_M, EPI_ROWS)]

    def gelu_epilogue(h):
        for rows in chunks():
            uv_ref[rows, _head(h)] = _gelu_exact(acc_ref[h, rows, :]).astype(BF16)

    def split4(src, h, dst_f32, dst_bf16):
        rows = TILE_M // 4
        for c in range(4):
            piece = src[h, pl.ds(c, rows, stride=4), :]
            if dst_f32 is not None:
                dst_f32[h, c * rows:(c + 1) * rows, :] = piece
            if dst_bf16 is not None:
                dst_bf16[c * rows:(c + 1) * rows, _head(h)] = piece.astype(BF16)

    def split16_from4(h, dst_bf16):
        rows4 = TILE_M // 4
        rows16 = TILE_M // 16
        for c_lo in range(4):
            for c_hi in range(4):
                c = 4 * c_hi + c_lo
                piece = t_ref[h, pl.ds(c_lo * rows4 + c_hi, rows16, stride=4), :]
                dst_bf16[c * rows16:(c + 1) * rows16, _head(h)] = piece.astype(BF16)

    def residue_layouts(src, h, dsts):
        need16 = 16 in dsts
        if 4 in dsts or need16:
            split4(src, h, t_ref if need16 else None, dsts.get(4))
        if need16:
            split16_from4(h, dsts[16])

    def qk_epilogue(gain, dsts, h):
        strided = any(r > 1 for r in dsts)
        w = h % EPI_SLOTS
        for rows in chunks():
            a = acc_ref[h, rows, :]
            ms = jnp.mean(a * a, axis=-1, keepdims=True)
            inv_ref[w, rows, :] = jnp.broadcast_to(lax.rsqrt(ms + EPS), (EPI_ROWS, HEAD_DIM))
        for rows in chunks():
            y_ref[w, rows, :] = acc_ref[h, rows, :] * inv_ref[w, rows, :] * gain
        for rows in chunks():
            r_ref[w, rows, :] = pltpu.roll(y_ref[w, rows, :], HEAD_DIM // 2, 1)
        for rows in chunks():
            y = y_ref[w, rows, :] * cos_ref[rows, :] + r_ref[w, rows, :] * sin_ref[rows, :]
            if 1 in dsts:
                dsts[1][rows, _head(h)] = y.astype(BF16)
            if strided:
                s_ref[h, rows, :] = y
        residue_layouts(s_ref, h, dsts)

    def v_epilogue(h):
        dsts = {r: v_ref.at[gi] for gi, r in enumerate(DILATIONS)}
        for rows in chunks():
            dsts[1][rows, _head(h)] = acc_ref[h, rows, :].astype(BF16)
        residue_layouts(acc_ref, h, dsts)

    q_gain = qg_ref[...] * (HEAD_DIM ** -0.5 * LOG2_E)
    epilogues = {}
    for b in range(IN_GELU_BLOCKS):
        epilogues[b] = gelu_epilogue
    for gi, r in enumerate(DILATIONS):
        epilogues[IN_Q_BLOCK0 + gi] = functools.partial(qk_epilogue, q_gain, {r: q_ref})
    epilogues[IN_K_BLOCK] = functools.partial(
        qk_epilogue, kg_ref[...], {r: k_ref.at[gi] for gi, r in enumerate(DILATIONS)})
    epilogues[IN_V_BLOCK] = v_epilogue

    @pl.when(t == 0)
    def _():
        normalize_rows()
        for j in range(IN_TILE_N // SUB_N):
            multiply(j)

    for b in range(IN_N_BLOCKS):
        @pl.when((n == b) & (t > 0))
        def _(b=b):
            for j in range(IN_TILE_N // SUB_N):
                for hh in range(HEADS_PER_SUB):
                    epilogues[(b - 1) % IN_N_BLOCKS](j * HEADS_PER_SUB + hh)
                multiply(j)
            if b == IN_N_BLOCKS - 1:
                normalize_rows()


def _in_proj(x2, gain, w_all, layer, cos_full, sin_signed, q_gain, k_gain, seq):
    m = x2.shape[0]
    row_tiles = m // TILE_M
    pos_blocks = seq // TILE_M
    nb = IN_N_BLOCKS
    steps = row_tiles * nb + 1

    mul_tile = lambda t: jnp.minimum((t + 1) // nb, row_tiles - 1)
    epi_tile = lambda t: jnp.maximum(t - 1, 0) // nb
    epi_blk = lambda t: jnp.maximum(t - 1, 0) % nb
    q_slot = lambda t: jnp.clip(epi_blk(t) - IN_Q_BLOCK0, 0, N_B_GROUPS - 1)
    return pl.pallas_call(
        _in_proj_kernel,
        grid=(steps,),
        in_specs=[
            pl.BlockSpec((TILE_M, D_MODEL), lambda t: (mul_tile(t), 0)),
            pl.BlockSpec((None, 1, D_MODEL), lambda t: (layer, 0, 0)),
            pl.BlockSpec((None, D_MODEL, IN_TILE_N), lambda t: (layer, 0, t % nb)),
            pl.BlockSpec((TILE_M, HEAD_DIM), lambda t: (epi_tile(t) % pos_blocks, 0)),
            pl.BlockSpec((TILE_M, HEAD_DIM), lambda t: (epi_tile(t) % pos_blocks, 0)),
            pl.BlockSpec((None, 1, HEAD_DIM), lambda t: (layer, 0, 0)),
            pl.BlockSpec((None, 1, HEAD_DIM), lambda t: (layer, 0, 0)),
        ],
        out_specs=[
            pl.BlockSpec((TILE_M, IN_TILE_N),
                         lambda t: (epi_tile(t), jnp.minimum(epi_blk(t), IN_GELU_BLOCKS - 1))),
            pl.BlockSpec((None, TILE_M, B_WIDTH), lambda t: (q_slot(t), epi_tile(t), 0)),
            pl.BlockSpec((N_B_GROUPS, TILE_M, B_WIDTH), lambda t: (0, epi_tile(t), 0)),
            pl.BlockSpec((N_B_GROUPS, TILE_M, B_WIDTH), lambda t: (0, epi_tile(t), 0)),
        ],
        out_shape=[
            jax.ShapeDtypeStruct((m, 2 * A_WIDTH), BF16),
            jax.ShapeDtypeStruct((N_B_GROUPS, m, B_WIDTH), BF16),
            jax.ShapeDtypeStruct((N_B_GROUPS, m, B_WIDTH), BF16),
            jax.ShapeDtypeStruct((N_B_GROUPS, m, B_WIDTH), BF16),
        ],
        scratch_shapes=[
            pltpu.VMEM((TILE_M, D_MODEL), BF16),
            pltpu.VMEM((N_B_HEADS, TILE_M, HEAD_DIM), F32),
            pltpu.VMEM((N_B_HEADS, TILE_M, HEAD_DIM), F32),
            pltpu.VMEM((N_B_HEADS, TILE_M, HEAD_DIM), F32),
            pltpu.VMEM((EPI_SLOTS, TILE_M, HEAD_DIM), F32),
            pltpu.VMEM((EPI_SLOTS, TILE_M, HEAD_DIM), F32),
            pltpu.VMEM((EPI_SLOTS, TILE_M, HEAD_DIM), F32),
        ],
        compiler_params=pltpu.CompilerParams(
            dimension_semantics=("arbitrary",),
            vmem_limit_bytes=VMEM_LIMIT),
        name="in_proj",
    )(x2, gain, w_all, cos_full, sin_signed, q_gain, k_gain)


def _gating_kernel(z_ref, lng_ref, lnb_ref, ws_ref, bs_ref, og_ref, o_ref,
                   wm_ref, vn_ref, a_ref, mu_ref, is_ref, mx_ref, rs_ref):
    row = lax.broadcasted_iota(jnp.int32, (CHUNK, CHUNK), 0)
    col = lax.broadcasted_iota(jnp.int32, (CHUNK, CHUNK), 1)
    for g in range(N_A_GROUPS):
        wm_ref[g] = jnp.where(row >= col, ws_ref[g], 0.0).astype(BF16)

    def chunk_rows(c):
        return slice(c * CHUNK, (c + 1) * CHUNK)

    def v_tile(c, g):
        return z_ref[chunk_rows(c), A_WIDTH + g * HEAD_DIM:A_WIDTH + (g + 1) * HEAD_DIM].astype(F32)

    tile = (CHUNK, HEAD_DIM)

    def ln_mean(c):
        for g in range(N_A_GROUPS):
            mu = jnp.mean(v_tile(c, g), axis=-1, keepdims=True)
            mu_ref[c % 2, g] = jnp.broadcast_to(mu, tile)

    def ln_var(c):
        for g in range(N_A_GROUPS):
            d = v_tile(c, g) - mu_ref[c % 2, g]
            var = jnp.mean(d * d, axis=-1, keepdims=True)
            is_ref[c % 2, g] = jnp.broadcast_to(lax.rsqrt(var + EPS), tile)

    def ln_apply(c):
        for g in range(N_A_GROUPS):
            vn = ((v_tile(c, g) - mu_ref[c % 2, g]) * is_ref[c % 2, g]
                  * lng_ref[g:g + 1, :] + lnb_ref[g:g + 1, :])
            vn_ref[chunk_rows(c), _head(g)] = vn.astype(BF16)

    def mix(c):
        for g in range(N_A_GROUPS):
            mx_ref[c % 2, g] = jnp.dot(wm_ref[g], vn_ref[chunk_rows(c), _head(g)],
                                       preferred_element_type=F32)

    def gate(c):
        sq = jnp.zeros(tile, F32)
        for g in range(N_A_GROUPS):
            a = z_ref[chunk_rows(c), _head(g)].astype(F32) * (mx_ref[c % 2, g] + bs_ref[g])
            sq = sq + a * a
            a_ref[chunk_rows(c), _head(g)] = a
        ms = jnp.sum(sq, axis=-1, keepdims=True) * (1.0 / A_WIDTH)
        rs_ref[c % 2] = jnp.broadcast_to(lax.rsqrt(ms + EPS), tile)

    def finish(c):
        for g in range(N_A_GROUPS):
            o_ref[chunk_rows(c), _head(g)] = (
                a_ref[chunk_rows(c), _head(g)] * rs_ref[c % 2] * og_ref[:, _head(g)]
            ).astype(BF16)

    stages = (ln_mean, ln_var, ln_apply, mix, gate, finish)
    n_chunks = GATE_ROWS // CHUNK
    for it in range(n_chunks + len(stages) - 1):
        for k in reversed(range(len(stages))):
            c = it - k
            if 0 <= c < n_chunks:
                stages[k](c)


def _gating(uv, ln_g, ln_b, w_s, b_s_b, out_gain):
    m = uv.shape[0]
    return pl.pallas_call(
        _gating_kernel,
        grid=(m // GATE_ROWS,),
        in_specs=[
            pl.BlockSpec((GATE_ROWS, 2 * A_WIDTH), lambda i: (i, 0)),
            pl.BlockSpec((N_A_GROUPS, HEAD_DIM), lambda i: (0, 0)),
            pl.BlockSpec((N_A_GROUPS, HEAD_DIM), lambda i: (0, 0)),
            pl.BlockSpec((N_A_GROUPS, CHUNK, CHUNK), lambda i: (0, 0, 0)),
            pl.BlockSpec((N_A_GROUPS, CHUNK, HEAD_DIM), lambda i: (0, 0, 0)),
            pl.BlockSpec((1, A_WIDTH), lambda i: (0, 0)),
        ],
        out_specs=pl.BlockSpec((GATE_ROWS, A_WIDTH), lambda i: (i, 0)),
        out_shape=jax.ShapeDtypeStruct((m, A_WIDTH), BF16),
        scratch_shapes=[
            pltpu.VMEM((N_A_GROUPS, CHUNK, CHUNK), BF16),
            pltpu.VMEM((GATE_ROWS, A_WIDTH), BF16),
            pltpu.VMEM((GATE_ROWS, A_WIDTH), F32),
            pltpu.VMEM((2, N_A_GROUPS, CHUNK, HEAD_DIM), F32),
            pltpu.VMEM((2, N_A_GROUPS, CHUNK, HEAD_DIM), F32),
            pltpu.VMEM((2, N_A_GROUPS, CHUNK, HEAD_DIM), F32),
            pltpu.VMEM((2, CHUNK, HEAD_DIM), F32),
        ],
        compiler_params=pltpu.CompilerParams(
            dimension_semantics=("parallel",), vmem_limit_bytes=VMEM_LIMIT),
        name="gating",
    )(uv, ln_g, ln_b, w_s, b_s_b, out_gain)


def _dilated_kernel(q_ref, k_ref, v_ref, o_ref, lse_ref, kk_ref, ve_ref,
                    sc_ref, m_ref, p_ref, *, piece_rows, att_rows):
    s = pl.program_id(2)
    n_pieces = att_rows // piece_rows
    win = QB + att_rows

    @pl.when(s == 0)
    def _():
        kk_ref[0:QB, :] = jnp.zeros((QB, B_WIDTH), BF16)
        ve_ref[:, 0:QB, 0:HEAD_DIM] = jnp.zeros((N_B_HEADS, QB, HEAD_DIM), BF16)
        ve_ref[:, :, HEAD_DIM:] = jnp.ones((N_B_HEADS, win, HEAD_DIM), BF16)

    @pl.when(s > 0)
    def _():
        kk_ref[0:QB, :] = kk_ref[att_rows:win, :]
        ve_ref[:, 0:QB, 0:HEAD_DIM] = ve_ref[:, att_rows:win, 0:HEAD_DIM]

    for p in range(n_pieces):
        rows = slice(QB + p * piece_rows, QB + (p + 1) * piece_rows)
        kk_ref[rows, :] = k_ref[p]
        for h in range(N_B_HEADS):
            ve_ref[h, rows, 0:HEAD_DIM] = v_ref[p, :, _head(h)]

    qi = lax.broadcasted_iota(jnp.int32, (QB, 2 * QB), 0)
    kj = lax.broadcasted_iota(jnp.int32, (QB, 2 * QB), 1)
    band = (kj >= qi) & (kj <= qi + QB)
    bias_band = jnp.where(band, 0.0, NEG_INF)
    bias_first = jnp.where(band & (kj >= QB), 0.0, NEG_INF)
    lane = lax.broadcasted_iota(jnp.int32, (QB, HEAD_DIM), 1)
    dn = (((1,), (1,)), ((), ()))

    def block_rows(ref, j, cols):
        if piece_rows >= QB:
            per = piece_rows // QB
            return ref[j // per, (j % per) * QB:(j % per + 1) * QB, cols]
        per = QB // piece_rows
        return jnp.concatenate([ref[j * per + t, :, cols] for t in range(per)], axis=0)

    def store_rows(ref, j, cols, val):
        if piece_rows >= QB:
            per = piece_rows // QB
            ref[j // per, (j % per) * QB:(j % per + 1) * QB, cols] = val
            return
        per = QB // piece_rows
        for t in range(per):
            ref[j * per + t, :, cols] = val[t * piece_rows:(t + 1) * piece_rows]

    units = [(j, h) for j in range(att_rows // QB) for h in range(N_B_HEADS)]

    def scores(u):
        j, h = units[u]
        q = block_rows(q_ref, j, _head(h))
        sc = lax.dot_general(q, kk_ref[j * QB:j * QB + 2 * QB, _head(h)], dn,
                             preferred_element_type=F32)
        bias = jnp.where(s == 0, bias_first, bias_band) if j == 0 else bias_band
        sc_ref[u] = sc + bias

    def row_max(u):
        m = jnp.max(sc_ref[u], axis=-1, keepdims=True)
        m_ref[u] = jnp.broadcast_to(m, (QB, HEAD_DIM))

    def probs(u):
        m = m_ref[u]
        for half in range(2):
            p_ref[u, :, _head(half)] = jnp.exp2(sc_ref[u, :, _head(half)] - m).astype(BF16)

    def weighted_values(u):
        j, h = units[u]
        oe = jnp.dot(p_ref[u], ve_ref[h, j * QB:j * QB + 2 * QB, :],
                     preferred_element_type=F32)
        den = oe[:, HEAD_DIM:]
        store_rows(o_ref, j, _head(h), (oe[:, :HEAD_DIM] / den).astype(BF16))
        m_ref[u] = m_ref[u] * LN_2 + jnp.log(den)

    stages = (scores, row_max, probs, weighted_values)
    for it in range(-(len(stages) - 1), len(units)):
        for depth, stage in enumerate(stages):
            u = it + len(stages) - 1 - depth
            if 0 <= u < len(units):
                stage(u)

    for j in range(att_rows // QB):
        tile = jnp.zeros((QB, HEAD_DIM), F32)
        for h in range(N_B_HEADS):
            tile = jnp.where(lane == h, m_ref[j * N_B_HEADS + h], tile)
        store_rows(lse_ref, j, slice(None), tile)


def _dilated_group(q_all, k_all, v_all, gi, dilation, batch, seq):
    r = dilation
    att_rows = min(MAX_ATT_ROWS, seq // r)
    units = (att_rows // QB) * N_B_HEADS
    piece_rows = min(TILE_M // r, att_rows)
    n_pieces = att_rows // piece_rows
    tiles = seq // (r * piece_rows)
    steps = tiles // n_pieces
    view = lambda a: a.reshape(N_B_GROUPS, batch, tiles, r, piece_rows, B_WIDTH)
    in_spec = pl.BlockSpec((None, None, n_pieces, None, piece_rows, B_WIDTH),
                           lambda b, c, s: (gi, b, s, c, 0, 0))
    o, lse = pl.pallas_call(
        functools.partial(_dilated_kernel, piece_rows=piece_rows, att_rows=att_rows),
        grid=(batch, r, steps),
        in_specs=[in_spec, in_spec, in_spec],
        out_specs=[
            pl.BlockSpec((None, n_pieces, None, piece_rows, B_WIDTH),
                         lambda b, c, s: (b, s, c, 0, 0)),
            pl.BlockSpec((None, n_pieces, None, piece_rows, HEAD_DIM),
                         lambda b, c, s: (b, s, c, 0, 0)),
        ],
        out_shape=[
            jax.ShapeDtypeStruct((batch, tiles, r, piece_rows, B_WIDTH), BF16),
            jax.ShapeDtypeStruct((batch, tiles, r, piece_rows, HEAD_DIM), F32),
        ],
        scratch_shapes=[
            pltpu.VMEM((QB + att_rows, B_WIDTH), BF16),
            pltpu.VMEM((N_B_HEADS, QB + att_rows, 2 * HEAD_DIM), BF16),
            pltpu.VMEM((units, QB, 2 * QB), F32),
            pltpu.VMEM((units, QB, HEAD_DIM), F32),
            pltpu.VMEM((units, QB, 2 * QB), BF16),
        ],
        compiler_params=pltpu.CompilerParams(
            dimension_semantics=("parallel", "parallel", "arbitrary"),
            vmem_limit_bytes=VMEM_LIMIT),
        name=f"dilated_r{r}",
    )(view(q_all), view(k_all), view(v_all))
    return (o.reshape(batch * seq, B_WIDTH), lse.reshape(batch * seq, HEAD_DIM))


def _out_proj_kernel(x_ref, a_ref, o0_ref, o1_ref, o2_ref, l0_ref, l1_ref, l2_ref,
                     bg_ref, w_ref, out_ref, mix_ref, b_ref, sq_ref, rs_ref, al_ref,
                     sp_ref, lslab_ref, oslab_ref):
    t = pl.program_id(0)
    o_refs = (o0_ref, o1_ref, o2_ref)
    l_refs = (l0_ref, l1_ref, l2_ref)
    n_sub = D_MODEL // OUT_SUB_N
    tile = (TILE_M, HEAD_DIM)

    def project(src, j):
        cols = slice(j * OUT_SUB_N, (j + 1) * OUT_SUB_N)
        out_ref[:, cols] = x_ref[:, cols] + jnp.dot(
            mix_ref[src], w_ref[:, cols], preferred_element_type=F32)

    def to_position_order(dst, src, r, cols):
        rows = TILE_M // r
        for c in range(r):
            dst[pl.ds(c, rows, stride=r), :] = src[c * rows:(c + 1) * rows, cols].astype(F32)

    def group_weights():
        lses = []
        for gi, r in enumerate(DILATIONS):
            if r == 1:
                lses.append(l_refs[gi][...])
            else:
                to_position_order(lslab_ref.at[gi], l_refs[gi], r, slice(None))
                lses.append(lslab_ref[gi])
        mx = jnp.maximum(jnp.maximum(lses[0], lses[1]), lses[2])
        es = [jnp.exp(l - mx) for l in lses]
        den = es[0] + es[1] + es[2]
        for gi, e in enumerate(es):
            al = e / den
            hi = al.astype(BF16)
            al_ref[gi, :, :HEAD_DIM] = hi
            al_ref[gi, :, HEAD_DIM:] = (al - hi.astype(F32)).astype(BF16)

    src_lane = lax.broadcasted_iota(jnp.int32, (2 * HEAD_DIM, 2 * HEAD_DIM), 0) % HEAD_DIM
    dst_head = lax.broadcasted_iota(jnp.int32, (2 * HEAD_DIM, 2 * HEAD_DIM), 1) // HEAD_DIM

    def spread_weights(h0):
        spread = jnp.where(src_lane == h0 + dst_head, 1.0, 0.0).astype(BF16)
        for gi in range(N_B_GROUPS):
            sp_ref[gi] = jnp.dot(al_ref[gi], spread, preferred_element_type=F32)

    def merge_head(h):
        b = None
        for gi, r in enumerate(DILATIONS):
            alpha = sp_ref[gi, :, _head(h % 2)]
            if r == 1:
                o = o_refs[gi][:, _head(h)].astype(F32)
            else:
                to_position_order(oslab_ref.at[gi, h % OUT_SLOTS], o_refs[gi], r, _head(h))
                o = oslab_ref[gi, h % OUT_SLOTS]
            b = alpha * o if b is None else b + alpha * o
        b_ref[:, _head(h)] = b
        sq_ref[...] = b * b if h == 0 else sq_ref[...] + b * b

    def inverse_rms():
        ms = jnp.sum(sq_ref[...], axis=-1, keepdims=True) * (1.0 / B_WIDTH)
        rs_ref[...] = jnp.broadcast_to(lax.rsqrt(ms + EPS), tile)

    def normalize(dst, h):
        cols = slice(A_WIDTH + h * HEAD_DIM, A_WIDTH + (h + 1) * HEAD_DIM)
        mix_ref[dst, :, cols] = (b_ref[:, _head(h)] * rs_ref[...] * bg_ref[:, _head(h)]).astype(BF16)

    def step(src, dst):
        subs = iter(range(n_sub))
        proj = (lambda: project(src, next(subs))) if src is not None else (lambda: None)
        proj()
        group_weights()
        mix_ref[dst, :, :A_WIDTH] = a_ref[...]
        proj()
        for h in range(N_B_HEADS):
            if h % 2 == 0:
                spread_weights(h)
            merge_head(h)
            if h % 2 == 1:
                proj()
        inverse_rms()
        for h in range(N_B_HEADS):
            normalize(dst, h)
            if h % 4 == 3:
                proj()

    pl.when(t == 0)(lambda: step(None, 0))
    for par in range(2):
        pl.when((t > 0) & (t % 2 == par))(functools.partial(step, 1 - par, par))


def _out_proj(x2, a_n, os_, lses, b_gain, w_all, layer):
    m = x2.shape[0]
    tiles = m // TILE_M
    done = lambda t: (jnp.maximum(t - 1, 0), 0)
    nxt = lambda t: (jnp.minimum(t, tiles - 1), 0)
    fixed = lambda t: (layer, 0, 0)
    return pl.pallas_call(
        _out_proj_kernel,
        grid=(tiles + 1,),
        in_specs=[
            pl.BlockSpec((TILE_M, D_MODEL), done),
            pl.BlockSpec((TILE_M, A_WIDTH), nxt),
            pl.BlockSpec((TILE_M, B_WIDTH), nxt),
            pl.BlockSpec((TILE_M, B_WIDTH), nxt),
            pl.BlockSpec((TILE_M, B_WIDTH), nxt),
            pl.BlockSpec((TILE_M, HEAD_DIM), nxt),
            pl.BlockSpec((TILE_M, HEAD_DIM), nxt),
            pl.BlockSpec((TILE_M, HEAD_DIM), nxt),
            pl.BlockSpec((None, 1, B_WIDTH), fixed),
            pl.BlockSpec((None, MIX_WIDTH, D_MODEL), fixed, pipeline_mode=pl.Buffered(1)),
        ],
        out_specs=pl.BlockSpec((TILE_M, D_MODEL), done),
        out_shape=jax.ShapeDtypeStruct((m, D_MODEL), F32),
        scratch_shapes=[
            pltpu.VMEM((2, TILE_M, MIX_WIDTH), BF16),
            pltpu.VMEM((TILE_M, B_WIDTH), F32),
            pltpu.VMEM((TILE_M, HEAD_DIM), F32),
            pltpu.VMEM((TILE_M, HEAD_DIM), F32),
            pltpu.VMEM((N_B_GROUPS, TILE_M, 2 * HEAD_DIM), BF16),
            pltpu.VMEM((N_B_GROUPS, TILE_M, 2 * HEAD_DIM), F32),
            pltpu.VMEM((N_B_GROUPS, TILE_M, HEAD_DIM), F32),
            pltpu.VMEM((N_B_GROUPS, OUT_SLOTS, TILE_M, HEAD_DIM), F32),
        ],
        compiler_params=pltpu.CompilerParams(
            dimension_semantics=("arbitrary",), vmem_limit_bytes=VMEM_LIMIT),
        name="out_proj",
    )(x2, a_n, *os_, *lses, b_gain, w_all)


def _ffn_kernel(x_ref, g_ref, wg_ref, wu_ref, wd_ref, out_ref, h_ref):
    f = pl.program_id(1)

    @pl.when(f == 0)
    def _():
        x = x_ref[...]
        h_ref[...] = _rms_normalize(x, g_ref[...]).astype(BF16)
        out_ref[...] = x

    h = h_ref[...]
    gate = jnp.dot(h, wg_ref[...], preferred_element_type=F32)
    up = jnp.dot(h, wu_ref[...], preferred_element_type=F32)
    act = (gate * jax.nn.sigmoid(gate) * up).astype(BF16)
    out_ref[...] += jnp.dot(act, wd_ref[...], preferred_element_type=F32)


def _ffn(x2, gain, w_gate, w_up, w_down, layer):
    m = x2.shape[0]
    return pl.pallas_call(
        _ffn_kernel,
        grid=(m // TILE_M, D_FF // FFN_TILE_F),
        in_specs=[
            pl.BlockSpec((TILE_M, D_MODEL), lambda i, f: (i, 0)),
            pl.BlockSpec((None, 1, D_MODEL), lambda i, f: (layer, 0, 0)),
            pl.BlockSpec((None, D_MODEL, FFN_TILE_F), lambda i, f: (layer, 0, f)),
            pl.BlockSpec((None, D_MODEL, FFN_TILE_F), lambda i, f: (layer, 0, f)),
            pl.BlockSpec((None, FFN_TILE_F, D_MODEL), lambda i, f: (layer, f, 0)),
        ],
        out_specs=pl.BlockSpec((TILE_M, D_MODEL), lambda i, f: (i, 0)),
        out_shape=jax.ShapeDtypeStruct((m, D_MODEL), F32),
        scratch_shapes=[pltpu.VMEM((TILE_M, D_MODEL), BF16)],
        compiler_params=pltpu.CompilerParams(
            dimension_semantics=("parallel", "arbitrary"),
            vmem_limit_bytes=VMEM_LIMIT),
        name="ffn",
    )(x2, gain, w_gate, w_up, w_down)


def _rope_tables(seq):
    pos = jnp.arange(seq, dtype=F32)
    inv_freq = 1.0 / (ROPE_THETA ** (jnp.arange(0, HEAD_DIM, 2, dtype=F32) / HEAD_DIM))
    ang = pos[:, None] * inv_freq[None, :]
    cos, sin = jnp.cos(ang), jnp.sin(ang)
    return (jnp.concatenate([cos, cos], axis=-1),
            jnp.concatenate([-sin, sin], axis=-1))


def kernel(x, mix_norm, w_in, a_ln_g, a_ln_b, a_w_s, a_b_s, q_norm, k_norm,
           a_out_norm, b_out_norm, w_out, ffn_norm, w_gate, w_up, w_down):
    batch, seq, d_model = x.shape
    depth = w_in.shape[0]
    assert d_model == D_MODEL and seq % TILE_M == 0
    assert all(w == r * QB for w, r in DILATED_CONFIGS)
    assert all(TILE_M % (r * 16) == 0 and seq % (r * 2 * QB) == 0 for r in DILATIONS)
    cos_full, sin_signed = _rope_tables(seq)
    x2 = x.reshape(batch * seq, d_model)
    w_in, w_out, w_gate, w_up, w_down = (
        w.astype(BF16) for w in (w_in, w_out, w_gate, w_up, w_down))
    as_rows = lambda p: p.reshape(depth, 1, p.shape[-1])
    mix_norm, q_norm, k_norm, b_out_norm, ffn_norm = (
        as_rows(p) for p in (mix_norm, q_norm, k_norm, b_out_norm, ffn_norm))
    for l in range(depth):
        uv, q_all, k_all, v_all = _in_proj(
            x2, mix_norm, w_in, l, cos_full, sin_signed, q_norm, k_norm, seq)
        b_s_b = jnp.broadcast_to(a_b_s[l][:, :, None], (N_A_GROUPS, CHUNK, HEAD_DIM))
        a_n = _gating(uv, a_ln_g[l], a_ln_b[l], a_w_s[l], b_s_b, a_out_norm[l][None])
        os_, lses = [], []
        for gi, r in enumerate(DILATIONS):
            o, lse = _dilated_group(q_all, k_all, v_all, gi, r, batch, seq)
            os_.append(o)
            lses.append(lse)
        x2 = _out_proj(x2, a_n, os_, lses, b_out_norm, w_out, l)
        x2 = _ffn(x2, ffn_norm, w_gate, w_up, w_down, l)
    return x2.reshape(batch, seq, d_model)
```

```python
import functools
import math

import jax
import jax.numpy as jnp
from jax import lax
from jax.experimental import pallas as pl
from jax.experimental.pallas import tpu as pltpu

D_MODEL = 2048
HEAD_DIM = 128
CHUNK = 128
QB = 128
N_A_GROUPS = 8
A_WIDTH = N_A_GROUPS * HEAD_DIM
N_B_HEADS = 8
DILATED_CONFIGS = ((128, 1), (512, 4), (2048, 16))
DILATIONS = tuple(r for _, r in DILATED_CONFIGS)
N_B_GROUPS = len(DILATED_CONFIGS)
B_WIDTH = N_B_HEADS * HEAD_DIM
MIX_WIDTH = A_WIDTH + B_WIDTH
IN_WIDTH = 2 * A_WIDTH + N_B_GROUPS * B_WIDTH + 2 * B_WIDTH
D_FF = -(-8 * D_MODEL // (3 * 256)) * 256
ROPE_THETA = 10000.0
EPS = 1e-6

IN_TILE_N = 1024
IN_N_BLOCKS = IN_WIDTH // IN_TILE_N
IN_GELU_BLOCKS = 2 * A_WIDTH // IN_TILE_N
IN_Q_BLOCK0 = IN_GELU_BLOCKS
IN_K_BLOCK = IN_Q_BLOCK0 + N_B_GROUPS
IN_V_BLOCK = IN_K_BLOCK + 1
IN_PAIR_STEPS = 2 * IN_N_BLOCKS
SUB_N = 256
HEADS_PER_SUB = SUB_N // HEAD_DIM

TILE_M = 512
EPI_ROWS = 64
EPI_SLOTS = 4
GATE_ROWS = 1024
FFN_TILE_M = 1024
FFN_TILE_F = 256
MAX_ATT_ROWS = 4 * QB
OUT_SUB_N = 256
OUT_SLOTS = 2
VMEM_LIMIT = 52 * 1024 * 1024

F32 = jnp.float32
BF16 = jnp.bfloat16
SQRT_HALF = float(math.sqrt(0.5))
LOG2_E = float(math.log2(math.e))
LN_2 = float(math.log(2.0))
NEG_INF = float("-inf")


def _rms_normalize(x, gain):
    ms = jnp.mean(x * x, axis=-1, keepdims=True)
    return x * lax.rsqrt(ms + EPS) * gain


def _gelu_exact(x):
    return 0.5 * x * (1.0 + lax.erf(x * SQRT_HALF))


def _head(h):
    return slice(h * HEAD_DIM, (h + 1) * HEAD_DIM)


def _in_proj_kernel(x_ref, g_ref, w_ref, cos_ref, sin_ref, qg_ref, kg_ref,
                    uv_ref, q_ref, k_ref, v_ref, h_ref, acc_ref, s_ref, t_ref,
                    inv_ref, y_ref, r_ref):
    t = pl.program_id(0)
    in_pair = t % IN_PAIR_STEPS
    half = t % 2

    def normalize_rows():
        h_ref[half] = _rms_normalize(x_ref[...], g_ref[...]).astype(BF16)

    def multiply(j):
        res = jnp.dot(h_ref[half], w_ref[:, j * SUB_N:(j + 1) * SUB_N],
                      preferred_element_type=F32)
        for hh in range(HEADS_PER_SUB):
            acc_ref[j * HEADS_PER_SUB + hh] = res[:, _head(hh)]

    def chunks():
        return [slice(c, c + EPI_ROWS) for c in range(0, TILE_M, EPI_ROWS)]

    def gelu_epilogue(h):
        for rows in chunks():
            uv_ref[rows, _head(h)] = _gelu_exact(acc_ref[h, rows, :]).astype(BF16)

    def split4(src, h, dst_f32, dst_bf16):
        rows = TILE_M // 4
        for c in range(4):
            piece = src[h, pl.ds(c, rows, stride=4), :]
            if dst_f32 is not None:
                dst_f32[h, c * rows:(c + 1) * rows, :] = piece
            if dst_bf16 is not None:
                dst_bf16[c * rows:(c + 1) * rows, _head(h)] = piece.astype(BF16)

    def split16_from4(h, dst_bf16):
        rows4 = TILE_M // 4
        rows16 = TILE_M // 16
        for c_lo in range(4):
            for c_hi in range(4):
                c = 4 * c_hi + c_lo
                piece = t_ref[h, pl.ds(c_lo * rows4 + c_hi, rows16, stride=4), :]
                dst_bf16[c * rows16:(c + 1) * rows16, _head(h)] = piece.astype(BF16)

    def residue_layouts(src, h, dsts):
        need16 = 16 in dsts
        if 4 in dsts or need16:
            split4(src, h, t_ref if need16 else None, dsts.get(4))
        if need16:
            split16_from4(h, dsts[16])

    def qk_epilogue(gain, dsts, h):
        strided = any(r > 1 for r in dsts)
        w = h % EPI_SLOTS
        for rows in chunks():
            a = acc_ref[h, rows, :]
            ms = jnp.mean(a * a, axis=-1, keepdims=True)
            inv_ref[w, rows, :] = jnp.broadcast_to(lax.rsqrt(ms + EPS), (EPI_ROWS, HEAD_DIM))
        for rows in chunks():
            y_ref[w, rows, :] = acc_ref[h, rows, :] * inv_ref[w, rows, :] * gain
        for rows in chunks():
            r_ref[w, rows, :] = pltpu.roll(y_ref[w, rows, :], HEAD_DIM // 2, 1)
        for rows in chunks():
            y = y_ref[w, rows, :] * cos_ref[rows, :] + r_ref[w, rows, :] * sin_ref[rows, :]
            if 1 in dsts:
                dsts[1][rows, _head(h)] = y.astype(BF16)
            if strided:
                s_ref[h, rows, :] = y
        residue_layouts(s_ref, h, dsts)

    def v_epilogue(h):
        dsts = {r: v_ref.at[gi] for gi, r in enumerate(DILATIONS)}
        for rows in chunks():
            dsts[1][rows, _head(h)] = acc_ref[h, rows, :].astype(BF16)
        residue_layouts(acc_ref, h, dsts)

    q_gain = qg_ref[...] * (HEAD_DIM ** -0.5 * LOG2_E)
    epilogues = {}
    for b in range(IN_GELU_BLOCKS):
        epilogues[b] = gelu_epilogue
    for gi, r in enumerate(DILATIONS):
        epilogues[IN_Q_BLOCK0 + gi] = functools.partial(qk_epilogue, q_gain, {r: q_ref})
    epilogues[IN_K_BLOCK] = functools.partial(
        qk_epilogue, kg_ref[...], {r: k_ref.at[gi] for gi, r in enumerate(DILATIONS)})
    epilogues[IN_V_BLOCK] = v_epilogue

    pl.when(t < 2)(normalize_rows)

    @pl.when(t == 0)
    def _():
        for j in range(IN_TILE_N // SUB_N):
            multiply(j)

    prev_block = ((t + IN_PAIR_STEPS - 1) % IN_PAIR_STEPS) // 2
    for b in range(IN_N_BLOCKS):
        @pl.when((prev_block == b) & (t > 0))
        def _(b=b):
            for j in range(IN_TILE_N // SUB_N):
                for hh in range(HEADS_PER_SUB):
                    epilogues[b](j * HEADS_PER_SUB + hh)
                multiply(j)

    pl.when(in_pair >= IN_PAIR_STEPS - 2)(normalize_rows)


def _in_proj(x2, gain, w_all, layer, cos_full, sin_signed, q_gain, k_gain, seq):
    m = x2.shape[0]
    row_tiles = m // TILE_M
    assert row_tiles % 2 == 0
    pos_blocks = seq // TILE_M
    ps = IN_PAIR_STEPS
    steps = (row_tiles // 2) * ps + 1

    def x_tile(t):
        nxt = 2 * (t // ps + 1) + (t % ps == ps - 1).astype(jnp.int32)
        return jnp.minimum(jnp.where(t < 2, t, nxt), row_tiles - 1)

    def epi(t):
        e = jnp.maximum(t - 1, 0)
        return 2 * (e // ps), e % ps

    def epi_tile(t):
        base, r = epi(t)
        return base + r % 2

    def uv_index(t):
        base, r = epi(t)
        r = jnp.minimum(r, 2 * IN_GELU_BLOCKS - 1)
        return base + r % 2, r // 2

    def q_index(t):
        base, r = epi(t)
        r = jnp.clip(r - 2 * IN_Q_BLOCK0, 0, 2 * N_B_GROUPS - 1)
        return r // 2, base + r % 2, 0

    def kv_index(blk):
        def index(t):
            base, r = epi(t)
            return 0, base + (r >= 2 * blk + 1).astype(jnp.int32), 0
        return index

    return pl.pallas_call(
        _in_proj_kernel,
        grid=(steps,),
        in_specs=[
            pl.BlockSpec((TILE_M, D_MODEL), lambda t: (x_tile(t), 0)),
            pl.BlockSpec((None, 1, D_MODEL), lambda t: (layer, 0, 0)),
            pl.BlockSpec((None, D_MODEL, IN_TILE_N), lambda t: (layer, 0, (t % ps) // 2)),
            pl.BlockSpec((TILE_M, HEAD_DIM), lambda t: (epi_tile(t) % pos_blocks, 0)),
            pl.BlockSpec((TILE_M, HEAD_DIM), lambda t: (epi_tile(t) % pos_blocks, 0)),
            pl.BlockSpec((None, 1, HEAD_DIM), lambda t: (layer, 0, 0)),
            pl.BlockSpec((None, 1, HEAD_DIM), lambda t: (layer, 0, 0)),
        ],
        out_specs=[
            pl.BlockSpec((TILE_M, IN_TILE_N), uv_index),
            pl.BlockSpec((None, TILE_M, B_WIDTH), q_index),
            pl.BlockSpec((N_B_GROUPS, TILE_M, B_WIDTH), kv_index(IN_K_BLOCK)),
            pl.BlockSpec((N_B_GROUPS, TILE_M, B_WIDTH), kv_index(IN_V_BLOCK)),
        ],
        out_shape=[
            jax.ShapeDtypeStruct((m, 2 * A_WIDTH), BF16),
            jax.ShapeDtypeStruct((N_B_GROUPS, m, B_WIDTH), BF16),
            jax.ShapeDtypeStruct((N_B_GROUPS, m, B_WIDTH), BF16),
            jax.ShapeDtypeStruct((N_B_GROUPS, m, B_WIDTH), BF16),
        ],
        scratch_shapes=[
            pltpu.VMEM((2, TILE_M, D_MODEL), BF16),
            pltpu.VMEM((N_B_HEADS, TILE_M, HEAD_DIM), F32),
            pltpu.VMEM((N_B_HEADS, TILE_M, HEAD_DIM), F32),
            pltpu.VMEM((N_B_HEADS, TILE_M, HEAD_DIM), F32),
            pltpu.VMEM((EPI_SLOTS, TILE_M, HEAD_DIM), F32),
            pltpu.VMEM((EPI_SLOTS, TILE_M, HEAD_DIM), F32),
            pltpu.VMEM((EPI_SLOTS, TILE_M, HEAD_DIM), F32),
        ],
        compiler_params=pltpu.CompilerParams(
            dimension_semantics=("arbitrary",),
            vmem_limit_bytes=VMEM_LIMIT),
        name="in_proj",
    )(x2, gain, w_all, cos_full, sin_signed, q_gain, k_gain)


def _gating_kernel(z_ref, lng_ref, lnb_ref, ws_ref, bs_ref, og_ref, o_ref,
                   wm_ref, vn_ref, a_ref, mu_ref, is_ref, mx_ref, rs_ref):
    row = lax.broadcasted_iota(jnp.int32, (CHUNK, CHUNK), 0)
    col = lax.broadcasted_iota(jnp.int32, (CHUNK, CHUNK), 1)
    for g in range(N_A_GROUPS):
        wm_ref[g] = jnp.where(row >= col, ws_ref[g], 0.0).astype(BF16)

    def chunk_rows(c):
        return slice(c * CHUNK, (c + 1) * CHUNK)

    def v_tile(c, g):
        return z_ref[chunk_rows(c), A_WIDTH + g * HEAD_DIM:A_WIDTH + (g + 1) * HEAD_DIM].astype(F32)

    tile = (CHUNK, HEAD_DIM)

    def ln_mean(c):
        for g in range(N_A_GROUPS):
            mu = jnp.mean(v_tile(c, g), axis=-1, keepdims=True)
            mu_ref[c % 2, g] = jnp.broadcast_to(mu, tile)

    def ln_var(c):
        for g in range(N_A_GROUPS):
            d = v_tile(c, g) - mu_ref[c % 2, g]
            var = jnp.mean(d * d, axis=-1, keepdims=True)
            is_ref[c % 2, g] = jnp.broadcast_to(lax.rsqrt(var + EPS), tile)

    def ln_apply(c):
        for g in range(N_A_GROUPS):
            vn = ((v_tile(c, g) - mu_ref[c % 2, g]) * is_ref[c % 2, g]
                  * lng_ref[g:g + 1, :] + lnb_ref[g:g + 1, :])
            vn_ref[chunk_rows(c), _head(g)] = vn.astype(BF16)

    def mix(c):
        for g in range(N_A_GROUPS):
            mx_ref[c % 2, g] = jnp.dot(wm_ref[g], vn_ref[chunk_rows(c), _head(g)],
                                       preferred_element_type=F32)

    def gate(c):
        sq = jnp.zeros(tile, F32)
        for g in range(N_A_GROUPS):
            a = z_ref[chunk_rows(c), _head(g)].astype(F32) * (mx_ref[c % 2, g] + bs_ref[g])
            sq = sq + a * a
            a_ref[chunk_rows(c), _head(g)] = a
        ms = jnp.sum(sq, axis=-1, keepdims=True) * (1.0 / A_WIDTH)
        rs_ref[c % 2] = jnp.broadcast_to(lax.rsqrt(ms + EPS), tile)

    def finish(c):
        for g in range(N_A_GROUPS):
            o_ref[chunk_rows(c), _head(g)] = (
                a_ref[chunk_rows(c), _head(g)] * rs_ref[c % 2] * og_ref[:, _head(g)]
            ).astype(BF16)

    stages = (ln_mean, ln_var, ln_apply, mix, gate, finish)
    n_chunks = GATE_ROWS // CHUNK
    for it in range(n_chunks + len(stages) - 1):
        for k in reversed(range(len(stages))):
            c = it - k
            if 0 <= c < n_chunks:
                stages[k](c)


def _gating(uv, ln_g, ln_b, w_s, b_s_b, out_gain):
    m = uv.shape[0]
    return pl.pallas_call(
        _gating_kernel,
        grid=(m // GATE_ROWS,),
        in_specs=[
            pl.BlockSpec((GATE_ROWS, 2 * A_WIDTH), lambda i: (i, 0)),
            pl.BlockSpec((N_A_GROUPS, HEAD_DIM), lambda i: (0, 0)),
            pl.BlockSpec((N_A_GROUPS, HEAD_DIM), lambda i: (0, 0)),
            pl.BlockSpec((N_A_GROUPS, CHUNK, CHUNK), lambda i: (0, 0, 0)),
            pl.BlockSpec((N_A_GROUPS, CHUNK, HEAD_DIM), lambda i: (0, 0, 0)),
            pl.BlockSpec((1, A_WIDTH), lambda i: (0, 0)),
        ],
        out_specs=pl.BlockSpec((GATE_ROWS, A_WIDTH), lambda i: (i, 0)),
        out_shape=jax.ShapeDtypeStruct((m, A_WIDTH), BF16),
        scratch_shapes=[
            pltpu.VMEM((N_A_GROUPS, CHUNK, CHUNK), BF16),
            pltpu.VMEM((GATE_ROWS, A_WIDTH), BF16),
            pltpu.VMEM((GATE_ROWS, A_WIDTH), F32),
            pltpu.VMEM((2, N_A_GROUPS, CHUNK, HEAD_DIM), F32),
            pltpu.VMEM((2, N_A_GROUPS, CHUNK, HEAD_DIM), F32),
            pltpu.VMEM((2, N_A_GROUPS, CHUNK, HEAD_DIM), F32),
            pltpu.VMEM((2, CHUNK, HEAD_DIM), F32),
        ],
        compiler_params=pltpu.CompilerParams(
            dimension_semantics=("parallel",), vmem_limit_bytes=VMEM_LIMIT),
        name="gating",
    )(uv, ln_g, ln_b, w_s, b_s_b, out_gain)


def _dilated_kernel(q_ref, k_ref, v_ref, o_ref, lse_ref, kk_ref, ve_ref,
                    sc_ref, m_ref, p_ref, *, piece_rows, att_rows):
    s = pl.program_id(2)
    n_pieces = att_rows // piece_rows
    win = QB + att_rows

    @pl.when(s == 0)
    def _():
        kk_ref[0:QB, :] = jnp.zeros((QB, B_WIDTH), BF16)
        ve_ref[:, 0:QB, 0:HEAD_DIM] = jnp.zeros((N_B_HEADS, QB, HEAD_DIM), BF16)
        ve_ref[:, :, HEAD_DIM:] = jnp.ones((N_B_HEADS, win, HEAD_DIM), BF16)

    @pl.when(s > 0)
    def _():
        kk_ref[0:QB, :] = kk_ref[att_rows:win, :]
        ve_ref[:, 0:QB, 0:HEAD_DIM] = ve_ref[:, att_rows:win, 0:HEAD_DIM]

    for p in range(n_pieces):
        rows = slice(QB + p * piece_rows, QB + (p + 1) * piece_rows)
        kk_ref[rows, :] = k_ref[p]
        for h in range(N_B_HEADS):
            ve_ref[h, rows, 0:HEAD_DIM] = v_ref[p, :, _head(h)]

    qi = lax.broadcasted_iota(jnp.int32, (QB, 2 * QB), 0)
    kj = lax.broadcasted_iota(jnp.int32, (QB, 2 * QB), 1)
    band = (kj >= qi) & (kj <= qi + QB)
    bias_band = jnp.where(band, 0.0, NEG_INF)
    bias_first = jnp.where(band & (kj >= QB), 0.0, NEG_INF)
    lane = lax.broadcasted_iota(jnp.int32, (QB, HEAD_DIM), 1)
    dn = (((1,), (1,)), ((), ()))

    def block_rows(ref, j, cols):
        if piece_rows >= QB:
            per = piece_rows // QB
            return ref[j // per, (j % per) * QB:(j % per + 1) * QB, cols]
        per = QB // piece_rows
        return jnp.concatenate([ref[j * per + t, :, cols] for t in range(per)], axis=0)

    def store_rows(ref, j, cols, val):
        if piece_rows >= QB:
            per = piece_rows // QB
            ref[j // per, (j % per) * QB:(j % per + 1) * QB, cols] = val
            return
        per = QB // piece_rows
        for t in range(per):
            ref[j * per + t, :, cols] = val[t * piece_rows:(t + 1) * piece_rows]

    units = [(j, h) for j in range(att_rows // QB) for h in range(N_B_HEADS)]

    def scores(u):
        j, h = units[u]
        q = block_rows(q_ref, j, _head(h))
        sc = lax.dot_general(q, kk_ref[j * QB:j * QB + 2 * QB, _head(h)], dn,
                             preferred_element_type=F32)
        bias = jnp.where(s == 0, bias_first, bias_band) if j == 0 else bias_band
        sc_ref[u] = sc + bias

    def row_max(u):
        m = jnp.max(sc_ref[u], axis=-1, keepdims=True)
        m_ref[u] = jnp.broadcast_to(m, (QB, HEAD_DIM))

    def probs(u):
        m = m_ref[u]
        for half in range(2):
            p_ref[u, :, _head(half)] = jnp.exp2(sc_ref[u, :, _head(half)] - m).astype(BF16)

    def weighted_values(u):
        j, h = units[u]
        oe = jnp.dot(p_ref[u], ve_ref[h, j * QB:j * QB + 2 * QB, :],
                     preferred_element_type=F32)
        den = oe[:, HEAD_DIM:]
        store_rows(o_ref, j, _head(h), (oe[:, :HEAD_DIM] / den).astype(BF16))
        m_ref[u] = m_ref[u] * LN_2 + jnp.log(den)

    stages = (scores, row_max, probs, weighted_values)
    for it in range(-(len(stages) - 1), len(units)):
        for depth, stage in enumerate(stages):
            u = it + len(stages) - 1 - depth
            if 0 <= u < len(units):
                stage(u)

    for j in range(att_rows // QB):
        tile = jnp.zeros((QB, HEAD_DIM), F32)
        for h in range(N_B_HEADS):
            tile = jnp.where(lane == h, m_ref[j * N_B_HEADS + h], tile)
        store_rows(lse_ref, j, slice(None), tile)


def _dilated_group(q_all, k_all, v_all, gi, dilation, batch, seq):
    r = dilation
    att_rows = min(MAX_ATT_ROWS, seq // r)
    units = (att_rows // QB) * N_B_HEADS
    piece_rows = min(TILE_M // r, att_rows)
    n_pieces = att_rows // piece_rows
    tiles = seq // (r * piece_rows)
    steps = tiles // n_pieces
    view = lambda a: a.reshape(N_B_GROUPS, batch, tiles, r, piece_rows, B_WIDTH)
    in_spec = pl.BlockSpec((None, None, n_pieces, None, piece_rows, B_WIDTH),
                           lambda b, c, s: (gi, b, s, c, 0, 0))
    o, lse = pl.pallas_call(
        functools.partial(_dilated_kernel, piece_rows=piece_rows, att_rows=att_rows),
        grid=(batch, r, steps),
        in_specs=[in_spec, in_spec, in_spec],
        out_specs=[
            pl.BlockSpec((None, n_pieces, None, piece_rows, B_WIDTH),
                         lambda b, c, s: (b, s, c, 0, 0)),
            pl.BlockSpec((None, n_pieces, None, piece_rows, HEAD_DIM),
                         lambda b, c, s: (b, s, c, 0, 0)),
        ],
        out_shape=[
            jax.ShapeDtypeStruct((batch, tiles, r, piece_rows, B_WIDTH), BF16),
            jax.ShapeDtypeStruct((batch, tiles, r, piece_rows, HEAD_DIM), F32),
        ],
        scratch_shapes=[
            pltpu.VMEM((QB + att_rows, B_WIDTH), BF16),
            pltpu.VMEM((N_B_HEADS, QB + att_rows, 2 * HEAD_DIM), BF16),
            pltpu.VMEM((units, QB, 2 * QB), F32),
            pltpu.VMEM((units, QB, HEAD_DIM), F32),
            pltpu.VMEM((units, QB, 2 * QB), BF16),
        ],
        compiler_params=pltpu.CompilerParams(
            dimension_semantics=("parallel", "parallel", "arbitrary"),
            vmem_limit_bytes=VMEM_LIMIT),
        name=f"dilated_r{r}",
    )(view(q_all), view(k_all), view(v_all))
    return (o.reshape(batch * seq, B_WIDTH), lse.reshape(batch * seq, HEAD_DIM))


def _out_proj_kernel(x_ref, a_ref, o0_ref, o1_ref, o2_ref, l0_ref, l1_ref, l2_ref,
                     bg_ref, w_ref, out_ref, mix_ref, b_ref, sq_ref, rs_ref, al_ref,
                     sp_ref, lslab_ref, oslab_ref):
    t = pl.program_id(0)
    o_refs = (o0_ref, o1_ref, o2_ref)
    l_refs = (l0_ref, l1_ref, l2_ref)
    n_sub = D_MODEL // OUT_SUB_N
    tile = (TILE_M, HEAD_DIM)

    def project(src, j):
        cols = slice(j * OUT_SUB_N, (j + 1) * OUT_SUB_N)
        out_ref[:, cols] = x_ref[:, cols] + jnp.dot(
            mix_ref[src], w_ref[:, cols], preferred_element_type=F32)

    def to_position_order(dst, src, r, cols):
        rows = TILE_M // r
        for c in range(r):
            dst[pl.ds(c, rows, stride=r), :] = src[c * rows:(c + 1) * rows, cols].astype(F32)

    def group_weights():
        lses = []
        for gi, r in enumerate(DILATIONS):
            if r == 1:
                lses.append(l_refs[gi][...])
            else:
                to_position_order(lslab_ref.at[gi], l_refs[gi], r, slice(None))
                lses.append(lslab_ref[gi])
        mx = jnp.maximum(jnp.maximum(lses[0], lses[1]), lses[2])
        es = [jnp.exp(l - mx) for l in lses]
        den = es[0] + es[1] + es[2]
        for gi, e in enumerate(es):
            al = e / den
            hi = al.astype(BF16)
            al_ref[gi, :, :HEAD_DIM] = hi
            al_ref[gi, :, HEAD_DIM:] = (al - hi.astype(F32)).astype(BF16)

    src_lane = lax.broadcasted_iota(jnp.int32, (2 * HEAD_DIM, 2 * HEAD_DIM), 0) % HEAD_DIM
    dst_head = lax.broadcasted_iota(jnp.int32, (2 * HEAD_DIM, 2 * HEAD_DIM), 1) // HEAD_DIM

    def spread_weights(h0):
        spread = jnp.where(src_lane == h0 + dst_head, 1.0, 0.0).astype(BF16)
        for gi in range(N_B_GROUPS):
            sp_ref[gi] = jnp.dot(al_ref[gi], spread, preferred_element_type=F32)

    def merge_head(h):
        b = None
        for gi, r in enumerate(DILATIONS):
            alpha = sp_ref[gi, :, _head(h % 2)]
            if r == 1:
                o = o_refs[gi][:, _head(h)].astype(F32)
            else:
                to_position_order(oslab_ref.at[gi, h % OUT_SLOTS], o_refs[gi], r, _head(h))
                o = oslab_ref[gi, h % OUT_SLOTS]
            b = alpha * o if b is None else b + alpha * o
        b_ref[:, _head(h)] = b
        sq_ref[...] = b * b if h == 0 else sq_ref[...] + b * b

    def inverse_rms():
        ms = jnp.sum(sq_ref[...], axis=-1, keepdims=True) * (1.0 / B_WIDTH)
        rs_ref[...] = jnp.broadcast_to(lax.rsqrt(ms + EPS), tile)

    def normalize(dst, h):
        cols = slice(A_WIDTH + h * HEAD_DIM, A_WIDTH + (h + 1) * HEAD_DIM)
        mix_ref[dst, :, cols] = (b_ref[:, _head(h)] * rs_ref[...] * bg_ref[:, _head(h)]).astype(BF16)

    def step(src, dst):
        subs = iter(range(n_sub))
        proj = (lambda: project(src, next(subs))) if src is not None else (lambda: None)
        proj()
        group_weights()
        mix_ref[dst, :, :A_WIDTH] = a_ref[...]
        proj()
        for h in range(N_B_HEADS):
            if h % 2 == 0:
                spread_weights(h)
            merge_head(h)
            if h % 2 == 1:
                proj()
        inverse_rms()
        for h in range(N_B_HEADS):
            normalize(dst, h)
            if h % 4 == 3:
                proj()

    pl.when(t == 0)(lambda: step(None, 0))
    for par in range(2):
        pl.when((t > 0) & (t % 2 == par))(functools.partial(step, 1 - par, par))


def _out_proj(x2, a_n, os_, lses, b_gain, w_all, layer):
    m = x2.shape[0]
    tiles = m // TILE_M
    done = lambda t: (jnp.maximum(t - 1, 0), 0)
    nxt = lambda t: (jnp.minimum(t, tiles - 1), 0)
    fixed = lambda t: (layer, 0, 0)
    return pl.pallas_call(
        _out_proj_kernel,
        grid=(tiles + 1,),
        in_specs=[
            pl.BlockSpec((TILE_M, D_MODEL), done),
            pl.BlockSpec((TILE_M, A_WIDTH), nxt),
            pl.BlockSpec((TILE_M, B_WIDTH), nxt),
            pl.BlockSpec((TILE_M, B_WIDTH), nxt),
            pl.BlockSpec((TILE_M, B_WIDTH), nxt),
            pl.BlockSpec((TILE_M, HEAD_DIM), nxt),
            pl.BlockSpec((TILE_M, HEAD_DIM), nxt),
            pl.BlockSpec((TILE_M, HEAD_DIM), nxt),
            pl.BlockSpec((None, 1, B_WIDTH), fixed),
            pl.BlockSpec((None, MIX_WIDTH, D_MODEL), fixed, pipeline_mode=pl.Buffered(1)),
        ],
        out_specs=pl.BlockSpec((TILE_M, D_MODEL), done),
        out_shape=jax.ShapeDtypeStruct((m, D_MODEL), F32),
        scratch_shapes=[
            pltpu.VMEM((2, TILE_M, MIX_WIDTH), BF16),
            pltpu.VMEM((TILE_M, B_WIDTH), F32),
            pltpu.VMEM((TILE_M, HEAD_DIM), F32),
            pltpu.VMEM((TILE_M, HEAD_DIM), F32),
            pltpu.VMEM((N_B_GROUPS, TILE_M, 2 * HEAD_DIM), BF16),
            pltpu.VMEM((N_B_GROUPS, TILE_M, 2 * HEAD_DIM), F32),
            pltpu.VMEM((N_B_GROUPS, TILE_M, HEAD_DIM), F32),
            pltpu.VMEM((N_B_GROUPS, OUT_SLOTS, TILE_M, HEAD_DIM), F32),
        ],
        compiler_params=pltpu.CompilerParams(
            dimension_semantics=("arbitrary",), vmem_limit_bytes=VMEM_LIMIT),
        name="out_proj",
    )(x2, a_n, *os_, *lses, b_gain, w_all)


def _ffn_kernel(x_ref, g_ref, wg_ref, wu_ref, wd_ref, out_ref, h_ref):
    f = pl.program_id(1)

    @pl.when(f == 0)
    def _():
        x = x_ref[...]
        h_ref[...] = _rms_normalize(x, g_ref[...]).astype(BF16)
        out_ref[...] = x

    h = h_ref[...]
    gate = jnp.dot(h, wg_ref[...], preferred_element_type=F32)
    up = jnp.dot(h, wu_ref[...], preferred_element_type=F32)
    act = (gate * jax.nn.sigmoid(gate) * up).astype(BF16)
    out_ref[...] += jnp.dot(act, wd_ref[...], preferred_element_type=F32)


def _ffn(x2, gain, w_gate, w_up, w_down, layer):
    m = x2.shape[0]
    return pl.pallas_call(
        _ffn_kernel,
        grid=(m // FFN_TILE_M, D_FF // FFN_TILE_F),
        in_specs=[
            pl.BlockSpec((FFN_TILE_M, D_MODEL), lambda i, f: (i, 0)),
            pl.BlockSpec((None, 1, D_MODEL), lambda i, f: (layer, 0, 0)),
            pl.BlockSpec((None, None, D_MODEL, FFN_TILE_F), lambda i, f: (layer, f, 0, 0)),
            pl.BlockSpec((None, None, D_MODEL, FFN_TILE_F), lambda i, f: (layer, f, 0, 0)),
            pl.BlockSpec((None, None, FFN_TILE_F, D_MODEL), lambda i, f: (layer, f, 0, 0)),
        ],
        out_specs=pl.BlockSpec((FFN_TILE_M, D_MODEL), lambda i, f: (i, 0)),
        out_shape=jax.ShapeDtypeStruct((m, D_MODEL), F32),
        scratch_shapes=[pltpu.VMEM((FFN_TILE_M, D_MODEL), BF16)],
        compiler_params=pltpu.CompilerParams(
            dimension_semantics=("parallel", "arbitrary"),
            vmem_limit_bytes=VMEM_LIMIT),
        name="ffn",
    )(x2, gain, w_gate, w_up, w_down)


def _rope_tables(seq):
    pos = jnp.arange(seq, dtype=F32)
    inv_freq = 1.0 / (ROPE_THETA ** (jnp.arange(0, HEAD_DIM, 2, dtype=F32) / HEAD_DIM))
    ang = pos[:, None] * inv_freq[None, :]
    cos, sin = jnp.cos(ang), jnp.sin(ang)
    return (jnp.concatenate([cos, cos], axis=-1),
            jnp.concatenate([-sin, sin], axis=-1))


def kernel(x, mix_norm, w_in, a_ln_g, a_ln_b, a_w_s, a_b_s, q_norm, k_norm,
           a_out_norm, b_out_norm, w_out, ffn_norm, w_gate, w_up, w_down):
    batch, seq, d_model = x.shape
    depth = w_in.shape[0]
    assert d_model == D_MODEL and seq % TILE_M == 0
    assert all(w == r * QB for w, r in DILATED_CONFIGS)
    assert all(TILE_M % (r * 16) == 0 and seq % (r * 2 * QB) == 0 for r in DILATIONS)
    cos_full, sin_signed = _rope_tables(seq)
    x2 = x.reshape(batch * seq, d_model)
    w_in, w_out, w_gate, w_up, w_down = (
        w.astype(BF16) for w in (w_in, w_out, w_gate, w_up, w_down))
    f_blocks = D_FF // FFN_TILE_F
    by_col_block = lambda w: w.reshape(depth, D_MODEL, f_blocks, FFN_TILE_F).transpose(0, 2, 1, 3)
    w_gate, w_up = by_col_block(w_gate), by_col_block(w_up)
    w_down = w_down.reshape(depth, f_blocks, FFN_TILE_F, D_MODEL)
    as_rows = lambda p: p.reshape(depth, 1, p.shape[-1])
    mix_norm, q_norm, k_norm, b_out_norm, ffn_norm = (
        as_rows(p) for p in (mix_norm, q_norm, k_norm, b_out_norm, ffn_norm))
    for l in range(depth):
        uv, q_all, k_all, v_all = _in_proj(
            x2, mix_norm, w_in, l, cos_full, sin_signed, q_norm, k_norm, seq)
        b_s_b = jnp.broadcast_to(a_b_s[l][:, :, None], (N_A_GROUPS, CHUNK, HEAD_DIM))
        a_n = _gating(uv, a_ln_g[l], a_ln_b[l], a_w_s[l], b_s_b, a_out_norm[l][None])
        os_, lses = [], []
        for gi, r in enumerate(DILATIONS):
            o, lse = _dilated_group(q_all, k_all, v_all, gi, r, batch, seq)
            os_.append(o)
            lses.append(lse)
        x2 = _out_proj(x2, a_n, os_, lses, b_out_norm, w_out, l)
        x2 = _ffn(x2, ffn_norm, w_gate, w_up, w_down, l)
    return x2.reshape(batch, seq, d_model)
```

```python
import functools
import math

import jax
import jax.numpy as jnp
from jax import lax
from jax.experimental import pallas as pl
from jax.experimental.pallas import tpu as pltpu

D_MODEL = 2048
HEAD_DIM = 128
CHUNK = 128
QB = 128
N_A_GROUPS = 8
A_WIDTH = N_A_GROUPS * HEAD_DIM
N_B_HEADS = 8
DILATED_CONFIGS = ((128, 1), (512, 4), (2048, 16))
DILATIONS = tuple(r for _, r in DILATED_CONFIGS)
N_B_GROUPS = len(DILATED_CONFIGS)
B_WIDTH = N_B_HEADS * HEAD_DIM
MIX_WIDTH = A_WIDTH + B_WIDTH
IN_WIDTH = 2 * A_WIDTH + N_B_GROUPS * B_WIDTH + 2 * B_WIDTH
D_FF = -(-8 * D_MODEL // (3 * 256)) * 256
ROPE_THETA = 10000.0
EPS = 1e-6

IN_TILE_N = 1024
IN_N_BLOCKS = IN_WIDTH // IN_TILE_N
IN_GELU_BLOCKS = 2 * A_WIDTH // IN_TILE_N
IN_Q_BLOCK0 = IN_GELU_BLOCKS
IN_K_BLOCK = IN_Q_BLOCK0 + N_B_GROUPS
IN_V_BLOCK = IN_K_BLOCK + 1
IN_PAIR_STEPS = 2 * IN_N_BLOCKS
SUB_N = 256
HEADS_PER_SUB = SUB_N // HEAD_DIM

TILE_M = 512
EPI_ROWS = 64
EPI_SLOTS = 4
GATE_ROWS = 1024
FFN_TILE_M = 512
FFN_TILE_F = 512
MAX_ATT_ROWS = 4 * QB
OUT_SUB_N = 256
OUT_SLOTS = 2
VMEM_LIMIT = 52 * 1024 * 1024

F32 = jnp.float32
BF16 = jnp.bfloat16
SQRT_HALF = float(math.sqrt(0.5))
LOG2_E = float(math.log2(math.e))
LN_2 = float(math.log(2.0))
NEG_INF = float("-inf")


def _rms_normalize(x, gain):
    ms = jnp.mean(x * x, axis=-1, keepdims=True)
    return x * lax.rsqrt(ms + EPS) * gain


def _gelu_exact(x):
    return 0.5 * x * (1.0 + lax.erf(x * SQRT_HALF))


def _head(h):
    return slice(h * HEAD_DIM, (h + 1) * HEAD_DIM)


def _in_proj_kernel(x_ref, g_ref, w_ref, cos_ref, sin_ref, qg_ref, kg_ref,
                    uv_ref, q_ref, k_ref, v_ref, h_ref, acc_ref, s_ref, t_ref,
                    inv_ref, y_ref, r_ref, hn_ref, *, pos_blocks):
    t = pl.program_id(0)
    in_pair = t % IN_PAIR_STEPS
    half = t % 2
    e = jnp.maximum(t - 1, 0)
    epi_tile = 2 * (e // IN_PAIR_STEPS) + e % 2
    pos0 = (epi_tile % pos_blocks) * TILE_M

    def rope_rows(ref, rows):
        return ref[pl.ds(pl.multiple_of(pos0 + rows.start, EPI_ROWS), EPI_ROWS), :]

    def normalize_rows():
        h_ref[half] = _rms_normalize(x_ref[...], g_ref[...]).astype(BF16)

    def normalize_part(j, parts):
        rows = slice(j * (TILE_M // parts), (j + 1) * (TILE_M // parts))
        hn_ref[rows, :] = _rms_normalize(x_ref[rows, :], g_ref[...]).astype(BF16)

    def multiply(j):
        res = jnp.dot(h_ref[half], w_ref[:, j * SUB_N:(j + 1) * SUB_N],
                      preferred_element_type=F32)
        for hh in range(HEADS_PER_SUB):
            acc_ref[j * HEADS_PER_SUB + hh] = res[:, _head(hh)]

    def chunks():
        return [slice(c, c + EPI_ROWS) for c in range(0, TILE_M, EPI_ROWS)]

    def gelu_epilogue(h):
        for rows in chunks():
            uv_ref[rows, _head(h)] = _gelu_exact(acc_ref[h, rows, :]).astype(BF16)

    def split4(src, h, dst_f32, dst_bf16):
        rows = TILE_M // 4
        for c in range(4):
            piece = src[h, pl.ds(c, rows, stride=4), :]
            if dst_f32 is not None:
                dst_f32[h, c * rows:(c + 1) * rows, :] = piece
            if dst_bf16 is not None:
                dst_bf16[c * rows:(c + 1) * rows, _head(h)] = piece.astype(BF16)

    def split16_from4(h, dst_bf16):
        rows4 = TILE_M // 4
        rows16 = TILE_M // 16
        for c_lo in range(4):
            for c_hi in range(4):
                c = 4 * c_hi + c_lo
                piece = t_ref[h, pl.ds(c_lo * rows4 + c_hi, rows16, stride=4), :]
                dst_bf16[c * rows16:(c + 1) * rows16, _head(h)] = piece.astype(BF16)

    def residue_layouts(src, h, dsts):
        need16 = 16 in dsts
        if 4 in dsts or need16:
            split4(src, h, t_ref if need16 else None, dsts.get(4))
        if need16:
            split16_from4(h, dsts[16])

    def qk_epilogue(gain, dsts, h):
        strided = any(r > 1 for r in dsts)
        w = h % EPI_SLOTS
        for rows in chunks():
            a = acc_ref[h, rows, :]
            ms = jnp.mean(a * a, axis=-1, keepdims=True)
            inv_ref[w, rows, :] = jnp.broadcast_to(lax.rsqrt(ms + EPS), (EPI_ROWS, HEAD_DIM))
        for rows in chunks():
            y_ref[w, rows, :] = acc_ref[h, rows, :] * inv_ref[w, rows, :] * gain
        for rows in chunks():
            r_ref[w, rows, :] = pltpu.roll(y_ref[w, rows, :], HEAD_DIM // 2, 1)
        for rows in chunks():
            y = (y_ref[w, rows, :] * rope_rows(cos_ref, rows)
                 + r_ref[w, rows, :] * rope_rows(sin_ref, rows))
            if 1 in dsts:
                dsts[1][rows, _head(h)] = y.astype(BF16)
            if strided:
                s_ref[h, rows, :] = y
        residue_layouts(s_ref, h, dsts)

    def v_epilogue(h):
        dsts = {r: v_ref.at[gi] for gi, r in enumerate(DILATIONS)}
        for rows in chunks():
            dsts[1][rows, _head(h)] = acc_ref[h, rows, :].astype(BF16)
        residue_layouts(acc_ref, h, dsts)

    q_gain = qg_ref[...] * (HEAD_DIM ** -0.5 * LOG2_E)
    epilogues = {}
    for b in range(IN_GELU_BLOCKS):
        epilogues[b] = gelu_epilogue
    for gi, r in enumerate(DILATIONS):
        epilogues[IN_Q_BLOCK0 + gi] = functools.partial(qk_epilogue, q_gain, {r: q_ref})
    epilogues[IN_K_BLOCK] = functools.partial(
        qk_epilogue, kg_ref[...], {r: k_ref.at[gi] for gi, r in enumerate(DILATIONS)})
    epilogues[IN_V_BLOCK] = v_epilogue

    pl.when(t < 2)(normalize_rows)

    @pl.when(t == 0)
    def _():
        for j in range(IN_TILE_N // SUB_N):
            multiply(j)

    prev_block = ((t + IN_PAIR_STEPS - 1) % IN_PAIR_STEPS) // 2
    last_multiply = in_pair >= IN_PAIR_STEPS - 2
    n_sub = IN_TILE_N // SUB_N
    for b in range(IN_N_BLOCKS):
        for renew in ((False, True) if b >= IN_K_BLOCK else (False,)):
            @pl.when((prev_block == b) & (t > 0) & (last_multiply == renew))
            def _(b=b, renew=renew):
                for j in range(n_sub):
                    for hh in range(HEADS_PER_SUB):
                        epilogues[b](j * HEADS_PER_SUB + hh)
                    multiply(j)
                    if renew:
                        normalize_part(j, n_sub)
                if renew:
                    h_ref[half] = hn_ref[...]


def _in_proj(x2, gain, w_all, layer, cos_full, sin_signed, q_gain, k_gain, seq):
    m = x2.shape[0]
    row_tiles = m // TILE_M
    assert row_tiles % 2 == 0
    pos_blocks = seq // TILE_M
    ps = IN_PAIR_STEPS
    steps = (row_tiles // 2) * ps + 1

    def x_tile(t):
        nxt = 2 * (t // ps + 1) + (t % ps == ps - 1).astype(jnp.int32)
        return jnp.minimum(jnp.where(t < 2, t, nxt), row_tiles - 1)

    def epi(t):
        e = jnp.maximum(t - 1, 0)
        return 2 * (e // ps), e % ps

    def uv_index(t):
        base, r = epi(t)
        r = jnp.minimum(r, 2 * IN_GELU_BLOCKS - 1)
        return base + r % 2, r // 2

    def q_index(t):
        base, r = epi(t)
        r = jnp.clip(r - 2 * IN_Q_BLOCK0, 0, 2 * N_B_GROUPS - 1)
        return r // 2, base + r % 2, 0

    def kv_index(blk):
        def index(t):
            base, r = epi(t)
            return 0, base + (r >= 2 * blk + 1).astype(jnp.int32), 0
        return index

    return pl.pallas_call(
        functools.partial(_in_proj_kernel, pos_blocks=pos_blocks),
        grid=(steps,),
        in_specs=[
            pl.BlockSpec((TILE_M, D_MODEL), lambda t: (x_tile(t), 0)),
            pl.BlockSpec((None, 1, D_MODEL), lambda t: (layer, 0, 0)),
            pl.BlockSpec((None, D_MODEL, IN_TILE_N), lambda t: (layer, 0, (t % ps) // 2)),
            pl.BlockSpec((seq, HEAD_DIM), lambda t: (0, 0), pipeline_mode=pl.Buffered(1)),
            pl.BlockSpec((seq, HEAD_DIM), lambda t: (0, 0), pipeline_mode=pl.Buffered(1)),
            pl.BlockSpec((None, 1, HEAD_DIM), lambda t: (layer, 0, 0)),
            pl.BlockSpec((None, 1, HEAD_DIM), lambda t: (layer, 0, 0)),
        ],
        out_specs=[
            pl.BlockSpec((TILE_M, IN_TILE_N), uv_index),
            pl.BlockSpec((None, TILE_M, B_WIDTH), q_index),
            pl.BlockSpec((N_B_GROUPS, TILE_M, B_WIDTH), kv_index(IN_K_BLOCK)),
            pl.BlockSpec((N_B_GROUPS, TILE_M, B_WIDTH), kv_index(IN_V_BLOCK)),
        ],
        out_shape=[
            jax.ShapeDtypeStruct((m, 2 * A_WIDTH), BF16),
            jax.ShapeDtypeStruct((N_B_GROUPS, m, B_WIDTH), BF16),
            jax.ShapeDtypeStruct((N_B_GROUPS, m, B_WIDTH), BF16),
            jax.ShapeDtypeStruct((N_B_GROUPS, m, B_WIDTH), BF16),
        ],
        scratch_shapes=[
            pltpu.VMEM((2, TILE_M, D_MODEL), BF16),
            pltpu.VMEM((N_B_HEADS, TILE_M, HEAD_DIM), F32),
            pltpu.VMEM((N_B_HEADS, TILE_M, HEAD_DIM), F32),
            pltpu.VMEM((N_B_HEADS, TILE_M, HEAD_DIM), F32),
            pltpu.VMEM((EPI_SLOTS, TILE_M, HEAD_DIM), F32),
            pltpu.VMEM((EPI_SLOTS, TILE_M, HEAD_DIM), F32),
            pltpu.VMEM((EPI_SLOTS, TILE_M, HEAD_DIM), F32),
            pltpu.VMEM((TILE_M, D_MODEL), BF16),
        ],
        compiler_params=pltpu.CompilerParams(
            dimension_semantics=("arbitrary",),
            vmem_limit_bytes=VMEM_LIMIT),
        name="in_proj",
    )(x2, gain, w_all, cos_full, sin_signed, q_gain, k_gain)


def _gating_kernel(z_ref, lng_ref, lnb_ref, ws_ref, bs_ref, og_ref, o_ref,
                   wm_ref, vn_ref, a_ref, mu_ref, is_ref, mx_ref, rs_ref):
    row = lax.broadcasted_iota(jnp.int32, (CHUNK, CHUNK), 0)
    col = lax.broadcasted_iota(jnp.int32, (CHUNK, CHUNK), 1)
    for g in range(N_A_GROUPS):
        wm_ref[g] = jnp.where(row >= col, ws_ref[g], 0.0).astype(BF16)

    def chunk_rows(c):
        return slice(c * CHUNK, (c + 1) * CHUNK)

    def v_tile(c, g):
        return z_ref[chunk_rows(c), A_WIDTH + g * HEAD_DIM:A_WIDTH + (g + 1) * HEAD_DIM].astype(F32)

    tile = (CHUNK, HEAD_DIM)

    def ln_mean(c):
        for g in range(N_A_GROUPS):
            mu = jnp.mean(v_tile(c, g), axis=-1, keepdims=True)
            mu_ref[c % 2, g] = jnp.broadcast_to(mu, tile)

    def ln_var(c):
        for g in range(N_A_GROUPS):
            d = v_tile(c, g) - mu_ref[c % 2, g]
            var = jnp.mean(d * d, axis=-1, keepdims=True)
            is_ref[c % 2, g] = jnp.broadcast_to(lax.rsqrt(var + EPS), tile)

    def ln_apply(c):
        for g in range(N_A_GROUPS):
            vn = ((v_tile(c, g) - mu_ref[c % 2, g]) * is_ref[c % 2, g]
                  * lng_ref[g:g + 1, :] + lnb_ref[g:g + 1, :])
            vn_ref[chunk_rows(c), _head(g)] = vn.astype(BF16)

    def mix(c):
        for g in range(N_A_GROUPS):
            mx_ref[c % 2, g] = jnp.dot(wm_ref[g], vn_ref[chunk_rows(c), _head(g)],
                                       preferred_element_type=F32)

    def gate(c):
        sq = jnp.zeros(tile, F32)
        for g in range(N_A_GROUPS):
            a = z_ref[chunk_rows(c), _head(g)].astype(F32) * (mx_ref[c % 2, g] + bs_ref[g])
            sq = sq + a * a
            a_ref[chunk_rows(c), _head(g)] = a
        ms = jnp.sum(sq, axis=-1, keepdims=True) * (1.0 / A_WIDTH)
        rs_ref[c % 2] = jnp.broadcast_to(lax.rsqrt(ms + EPS), tile)

    def finish(c):
        for g in range(N_A_GROUPS):
            o_ref[chunk_rows(c), _head(g)] = (
                a_ref[chunk_rows(c), _head(g)] * rs_ref[c % 2] * og_ref[:, _head(g)]
            ).astype(BF16)

    stages = (ln_mean, ln_var, ln_apply, mix, gate, finish)
    n_chunks = GATE_ROWS // CHUNK
    for it in range(n_chunks + len(stages) - 1):
        for k in reversed(range(len(stages))):
            c = it - k
            if 0 <= c < n_chunks:
                stages[k](c)


def _gating(uv, ln_g, ln_b, w_s, b_s_b, out_gain):
    m = uv.shape[0]
    return pl.pallas_call(
        _gating_kernel,
        grid=(m // GATE_ROWS,),
        in_specs=[
            pl.BlockSpec((GATE_ROWS, 2 * A_WIDTH), lambda i: (i, 0)),
            pl.BlockSpec((N_A_GROUPS, HEAD_DIM), lambda i: (0, 0)),
            pl.BlockSpec((N_A_GROUPS, HEAD_DIM), lambda i: (0, 0)),
            pl.BlockSpec((N_A_GROUPS, CHUNK, CHUNK), lambda i: (0, 0, 0)),
            pl.BlockSpec((N_A_GROUPS, CHUNK, HEAD_DIM), lambda i: (0, 0, 0)),
            pl.BlockSpec((1, A_WIDTH), lambda i: (0, 0)),
        ],
        out_specs=pl.BlockSpec((GATE_ROWS, A_WIDTH), lambda i: (i, 0)),
        out_shape=jax.ShapeDtypeStruct((m, A_WIDTH), BF16),
        scratch_shapes=[
            pltpu.VMEM((N_A_GROUPS, CHUNK, CHUNK), BF16),
            pltpu.VMEM((GATE_ROWS, A_WIDTH), BF16),
            pltpu.VMEM((GATE_ROWS, A_WIDTH), F32),
            pltpu.VMEM((2, N_A_GROUPS, CHUNK, HEAD_DIM), F32),
            pltpu.VMEM((2, N_A_GROUPS, CHUNK, HEAD_DIM), F32),
            pltpu.VMEM((2, N_A_GROUPS, CHUNK, HEAD_DIM), F32),
            pltpu.VMEM((2, CHUNK, HEAD_DIM), F32),
        ],
        compiler_params=pltpu.CompilerParams(
            dimension_semantics=("parallel",), vmem_limit_bytes=VMEM_LIMIT),
        name="gating",
    )(uv, ln_g, ln_b, w_s, b_s_b, out_gain)


def _dilated_kernel(q_ref, k_ref, v_ref, o_ref, lse_ref, kk_ref, ve_ref,
                    sc_ref, m_ref, p_ref, *, piece_rows, att_rows):
    s = pl.program_id(2)
    n_pieces = att_rows // piece_rows
    win = QB + att_rows

    @pl.when(s == 0)
    def _():
        kk_ref[0:QB, :] = jnp.zeros((QB, B_WIDTH), BF16)
        ve_ref[:, 0:QB, 0:HEAD_DIM] = jnp.zeros((N_B_HEADS, QB, HEAD_DIM), BF16)
        ve_ref[:, :, HEAD_DIM:] = jnp.ones((N_B_HEADS, win, HEAD_DIM), BF16)

    @pl.when(s > 0)
    def _():
        kk_ref[0:QB, :] = kk_ref[att_rows:win, :]
        ve_ref[:, 0:QB, 0:HEAD_DIM] = ve_ref[:, att_rows:win, 0:HEAD_DIM]

    for p in range(n_pieces):
        rows = slice(QB + p * piece_rows, QB + (p + 1) * piece_rows)
        kk_ref[rows, :] = k_ref[p]
        for h in range(N_B_HEADS):
            ve_ref[h, rows, 0:HEAD_DIM] = v_ref[p, :, _head(h)]

    qi = lax.broadcasted_iota(jnp.int32, (QB, 2 * QB), 0)
    kj = lax.broadcasted_iota(jnp.int32, (QB, 2 * QB), 1)
    band = (kj >= qi) & (kj <= qi + QB)
    bias_band = jnp.where(band, 0.0, NEG_INF)
    bias_first = jnp.where(band & (kj >= QB), 0.0, NEG_INF)
    lane = lax.broadcasted_iota(jnp.int32, (QB, HEAD_DIM), 1)
    dn = (((1,), (1,)), ((), ()))

    def block_rows(ref, j, cols):
        if piece_rows >= QB:
            per = piece_rows // QB
            return ref[j // per, (j % per) * QB:(j % per + 1) * QB, cols]
        per = QB // piece_rows
        return jnp.concatenate([ref[j * per + t, :, cols] for t in range(per)], axis=0)

    def store_rows(ref, j, cols, val):
        if piece_rows >= QB:
            per = piece_rows // QB
            ref[j // per, (j % per) * QB:(j % per + 1) * QB, cols] = val
            return
        per = QB // piece_rows
        for t in range(per):
            ref[j * per + t, :, cols] = val[t * piece_rows:(t + 1) * piece_rows]

    units = [(j, h) for j in range(att_rows // QB) for h in range(N_B_HEADS)]

    def scores(u):
        j, h = units[u]
        q = block_rows(q_ref, j, _head(h))
        sc = lax.dot_general(q, kk_ref[j * QB:j * QB + 2 * QB, _head(h)], dn,
                             preferred_element_type=F32)
        bias = jnp.where(s == 0, bias_first, bias_band) if j == 0 else bias_band
        sc_ref[u] = sc + bias

    def row_max(u):
        m = jnp.max(sc_ref[u], axis=-1, keepdims=True)
        m_ref[u] = jnp.broadcast_to(m, (QB, HEAD_DIM))

    def probs(u):
        m = m_ref[u]
        for half in range(2):
            p_ref[u, :, _head(half)] = jnp.exp2(sc_ref[u, :, _head(half)] - m).astype(BF16)

    def weighted_values(u):
        j, h = units[u]
        oe = jnp.dot(p_ref[u], ve_ref[h, j * QB:j * QB + 2 * QB, :],
                     preferred_element_type=F32)
        den = oe[:, HEAD_DIM:]
        store_rows(o_ref, j, _head(h), (oe[:, :HEAD_DIM] / den).astype(BF16))
        m_ref[u] = m_ref[u] * LN_2 + jnp.log(den)

    stages = (scores, row_max, probs, weighted_values)
    for it in range(-(len(stages) - 1), len(units)):
        for depth, stage in enumerate(stages):
            u = it + len(stages) - 1 - depth
            if 0 <= u < len(units):
                stage(u)

    for j in range(att_rows // QB):
        tile = jnp.zeros((QB, HEAD_DIM), F32)
        for h in range(N_B_HEADS):
            tile = jnp.where(lane == h, m_ref[j * N_B_HEADS + h], tile)
        store_rows(lse_ref, j, slice(None), tile)


def _dilated_group(q_all, k_all, v_all, gi, dilation, batch, seq):
    r = dilation
    att_rows = min(MAX_ATT_ROWS, seq // r)
    units = (att_rows // QB) * N_B_HEADS
    piece_rows = min(TILE_M // r, att_rows)
    n_pieces = att_rows // piece_rows
    tiles = seq // (r * piece_rows)
    steps = tiles // n_pieces
    view = lambda a: a.reshape(N_B_GROUPS, batch, tiles, r, piece_rows, B_WIDTH)
    in_spec = pl.BlockSpec((None, None, n_pieces, None, piece_rows, B_WIDTH),
                           lambda b, c, s: (gi, b, s, c, 0, 0))
    o, lse = pl.pallas_call(
        functools.partial(_dilated_kernel, piece_rows=piece_rows, att_rows=att_rows),
        grid=(batch, r, steps),
        in_specs=[in_spec, in_spec, in_spec],
        out_specs=[
            pl.BlockSpec((None, n_pieces, None, piece_rows, B_WIDTH),
                         lambda b, c, s: (b, s, c, 0, 0)),
            pl.BlockSpec((None, n_pieces, None, piece_rows, HEAD_DIM),
                         lambda b, c, s: (b, s, c, 0, 0)),
        ],
        out_shape=[
            jax.ShapeDtypeStruct((batch, tiles, r, piece_rows, B_WIDTH), BF16),
            jax.ShapeDtypeStruct((batch, tiles, r, piece_rows, HEAD_DIM), F32),
        ],
        scratch_shapes=[
            pltpu.VMEM((QB + att_rows, B_WIDTH), BF16),
            pltpu.VMEM((N_B_HEADS, QB + att_rows, 2 * HEAD_DIM), BF16),
            pltpu.VMEM((units, QB, 2 * QB), F32),
            pltpu.VMEM((units, QB, HEAD_DIM), F32),
            pltpu.VMEM((units, QB, 2 * QB), BF16),
        ],
        compiler_params=pltpu.CompilerParams(
            dimension_semantics=("parallel", "parallel", "arbitrary"),
            vmem_limit_bytes=VMEM_LIMIT),
        name=f"dilated_r{r}",
    )(view(q_all), view(k_all), view(v_all))
    return (o.reshape(batch * seq, B_WIDTH), lse.reshape(batch * seq, HEAD_DIM))


def _out_proj_kernel(x_ref, a_ref, o0_ref, o1_ref, o2_ref, l0_ref, l1_ref, l2_ref,
                     bg_ref, w_ref, out_ref, mix_ref, b_ref, sq_ref, rs_ref, al_ref,
                     sp_ref, lslab_ref, oslab_ref):
    t = pl.program_id(0)
    o_refs = (o0_ref, o1_ref, o2_ref)
    l_refs = (l0_ref, l1_ref, l2_ref)
    n_sub = D_MODEL // OUT_SUB_N
    tile = (TILE_M, HEAD_DIM)

    def project(src, j):
        cols = slice(j * OUT_SUB_N, (j + 1) * OUT_SUB_N)
        out_ref[:, cols] = x_ref[:, cols] + jnp.dot(
            mix_ref[src], w_ref[:, cols], preferred_element_type=F32)

    def to_position_order(dst, src, r, cols):
        rows = TILE_M // r
        for c in range(r):
            dst[pl.ds(c, rows, stride=r), :] = src[c * rows:(c + 1) * rows, cols].astype(F32)

    def group_weights():
        lses = []
        for gi, r in enumerate(DILATIONS):
            if r == 1:
                lses.append(l_refs[gi][...])
            else:
                to_position_order(lslab_ref.at[gi], l_refs[gi], r, slice(None))
                lses.append(lslab_ref[gi])
        mx = jnp.maximum(jnp.maximum(lses[0], lses[1]), lses[2])
        es = [jnp.exp(l - mx) for l in lses]
        den = es[0] + es[1] + es[2]
        for gi, e in enumerate(es):
            al = e / den
            hi = al.astype(BF16)
            al_ref[gi, :, :HEAD_DIM] = hi
            al_ref[gi, :, HEAD_DIM:] = (al - hi.astype(F32)).astype(BF16)

    src_lane = lax.broadcasted_iota(jnp.int32, (2 * HEAD_DIM, 2 * HEAD_DIM), 0) % HEAD_DIM
    dst_head = lax.broadcasted_iota(jnp.int32, (2 * HEAD_DIM, 2 * HEAD_DIM), 1) // HEAD_DIM

    def spread_weights(h0):
        spread = jnp.where(src_lane == h0 + dst_head, 1.0, 0.0).astype(BF16)
        for gi in range(N_B_GROUPS):
            sp_ref[gi] = jnp.dot(al_ref[gi], spread, preferred_element_type=F32)

    def merge_head(h):
        b = None
        for gi, r in enumerate(DILATIONS):
            alpha = sp_ref[gi, :, _head(h % 2)]
            if r == 1:
                o = o_refs[gi][:, _head(h)].astype(F32)
            else:
                to_position_order(oslab_ref.at[gi, h % OUT_SLOTS], o_refs[gi], r, _head(h))
                o = oslab_ref[gi, h % OUT_SLOTS]
            b = alpha * o if b is None else b + alpha * o
        b_ref[:, _head(h)] = b
        sq_ref[...] = b * b if h == 0 else sq_ref[...] + b * b

    def inverse_rms():
        ms = jnp.sum(sq_ref[...], axis=-1, keepdims=True) * (1.0 / B_WIDTH)
        rs_ref[...] = jnp.broadcast_to(lax.rsqrt(ms + EPS), tile)

    def normalize(dst, h):
        cols = slice(A_WIDTH + h * HEAD_DIM, A_WIDTH + (h + 1) * HEAD_DIM)
        mix_ref[dst, :, cols] = (b_ref[:, _head(h)] * rs_ref[...] * bg_ref[:, _head(h)]).astype(BF16)

    def step(src, dst):
        subs = iter(range(n_sub))
        proj = (lambda: project(src, next(subs))) if src is not None else (lambda: None)
        proj()
        group_weights()
        mix_ref[dst, :, :A_WIDTH] = a_ref[...]
        proj()
        for h in range(N_B_HEADS):
            if h % 2 == 0:
                spread_weights(h)
            merge_head(h)
            if h % 2 == 1:
                proj()
        inverse_rms()
        for h in range(N_B_HEADS):
            normalize(dst, h)
            if h % 4 == 3:
                proj()

    pl.when(t == 0)(lambda: step(None, 0))
    for par in range(2):
        pl.when((t > 0) & (t % 2 == par))(functools.partial(step, 1 - par, par))


def _out_proj(x2, a_n, os_, lses, b_gain, w_all, layer):
    m = x2.shape[0]
    tiles = m // TILE_M
    done = lambda t: (jnp.maximum(t - 1, 0), 0)
    nxt = lambda t: (jnp.minimum(t, tiles - 1), 0)
    fixed = lambda t: (layer, 0, 0)
    return pl.pallas_call(
        _out_proj_kernel,
        grid=(tiles + 1,),
        in_specs=[
            pl.BlockSpec((TILE_M, D_MODEL), done),
            pl.BlockSpec((TILE_M, A_WIDTH), nxt),
            pl.BlockSpec((TILE_M, B_WIDTH), nxt),
            pl.BlockSpec((TILE_M, B_WIDTH), nxt),
            pl.BlockSpec((TILE_M, B_WIDTH), nxt),
            pl.BlockSpec((TILE_M, HEAD_DIM), nxt),
            pl.BlockSpec((TILE_M, HEAD_DIM), nxt),
            pl.BlockSpec((TILE_M, HEAD_DIM), nxt),
            pl.BlockSpec((None, 1, B_WIDTH), fixed),
            pl.BlockSpec((None, MIX_WIDTH, D_MODEL), fixed, pipeline_mode=pl.Buffered(1)),
        ],
        out_specs=pl.BlockSpec((TILE_M, D_MODEL), done),
        out_shape=jax.ShapeDtypeStruct((m, D_MODEL), F32),
        scratch_shapes=[
            pltpu.VMEM((2, TILE_M, MIX_WIDTH), BF16),
            pltpu.VMEM((TILE_M, B_WIDTH), F32),
            pltpu.VMEM((TILE_M, HEAD_DIM), F32),
            pltpu.VMEM((TILE_M, HEAD_DIM), F32),
            pltpu.VMEM((N_B_GROUPS, TILE_M, 2 * HEAD_DIM), BF16),
            pltpu.VMEM((N_B_GROUPS, TILE_M, 2 * HEAD_DIM), F32),
            pltpu.VMEM((N_B_GROUPS, TILE_M, HEAD_DIM), F32),
            pltpu.VMEM((N_B_GROUPS, OUT_SLOTS, TILE_M, HEAD_DIM), F32),
        ],
        compiler_params=pltpu.CompilerParams(
            dimension_semantics=("arbitrary",), vmem_limit_bytes=VMEM_LIMIT),
        name="out_proj",
    )(x2, a_n, *os_, *lses, b_gain, w_all)


def _ffn_kernel(x_ref, g_ref, wg_ref, wu_ref, wd_ref, out_ref, h_ref):
    f = pl.program_id(1)

    @pl.when(f == 0)
    def _():
        x = x_ref[...]
        h_ref[...] = _rms_normalize(x, g_ref[...]).astype(BF16)
        out_ref[...] = x

    h = h_ref[...]
    gate = jnp.dot(h, wg_ref[...], preferred_element_type=F32)
    up = jnp.dot(h, wu_ref[...], preferred_element_type=F32)
    act = (gate * jax.nn.sigmoid(gate) * up).astype(BF16)
    out_ref[...] += jnp.dot(act, wd_ref[...], preferred_element_type=F32)


def _ffn(x2, gain, w_gate, w_up, w_down, layer):
    m = x2.shape[0]
    return pl.pallas_call(
        _ffn_kernel,
        grid=(m // FFN_TILE_M, D_FF // FFN_TILE_F),
        in_specs=[
            pl.BlockSpec((FFN_TILE_M, D_MODEL), lambda i, f: (i, 0)),
            pl.BlockSpec((None, 1, D_MODEL), lambda i, f: (layer, 0, 0)),
            pl.BlockSpec((None, D_MODEL, FFN_TILE_F), lambda i, f: (layer, 0, f)),
            pl.BlockSpec((None, D_MODEL, FFN_TILE_F), lambda i, f: (layer, 0, f)),
            pl.BlockSpec((None, FFN_TILE_F, D_MODEL), lambda i, f: (layer, f, 0)),
        ],
        out_specs=pl.BlockSpec((FFN_TILE_M, D_MODEL), lambda i, f: (i, 0)),
        out_shape=jax.ShapeDtypeStruct((m, D_MODEL), F32),
        scratch_shapes=[pltpu.VMEM((FFN_TILE_M, D_MODEL), BF16)],
        compiler_params=pltpu.CompilerParams(
            dimension_semantics=("parallel", "arbitrary"),
            vmem_limit_bytes=VMEM_LIMIT),
        name="ffn",
    )(x2, gain, w_gate, w_up, w_down)


def _rope_tables(seq):
    pos = jnp.arange(seq, dtype=F32)
    inv_freq = 1.0 / (ROPE_THETA ** (jnp.arange(0, HEAD_DIM, 2, dtype=F32) / HEAD_DIM))
    ang = pos[:, None] * inv_freq[None, :]
    cos, sin = jnp.cos(ang), jnp.sin(ang)
    return (jnp.concatenate([cos, cos], axis=-1),
            jnp.concatenate([-sin, sin], axis=-1))


def kernel(x, mix_norm, w_in, a_ln_g, a_ln_b, a_w_s, a_b_s, q_norm, k_norm,
           a_out_norm, b_out_norm, w_out, ffn_norm, w_gate, w_up, w_down):
    batch, seq, d_model = x.shape
    depth = w_in.shape[0]
    assert d_model == D_MODEL and seq % TILE_M == 0
    assert all(w == r * QB for w, r in DILATED_CONFIGS)
    assert all(TILE_M % (r * 16) == 0 and seq % (r * 2 * QB) == 0 for r in DILATIONS)
    cos_full, sin_signed = _rope_tables(seq)
    x2 = x.reshape(batch * seq, d_model)
    w_in, w_out, w_gate, w_up, w_down = (
        w.astype(BF16) for w in (w_in, w_out, w_gate, w_up, w_down))
    as_rows = lambda p: p.reshape(depth, 1, p.shape[-1])
    mix_norm, q_norm, k_norm, b_out_norm, ffn_norm = (
        as_rows(p) for p in (mix_norm, q_norm, k_norm, b_out_norm, ffn_norm))
    for l in range(depth):
        uv, q_all, k_all, v_all = _in_proj(
            x2, mix_norm, w_in, l, cos_full, sin_signed, q_norm, k_norm, seq)
        b_s_b = jnp.broadcast_to(a_b_s[l][:, :, None], (N_A_GROUPS, CHUNK, HEAD_DIM))
        a_n = _gating(uv, a_ln_g[l], a_ln_b[l], a_w_s[l], b_s_b, a_out_norm[l][None])
        os_, lses = [], []
        for gi, r in enumerate(DILATIONS):
            o, lse = _dilated_group(q_all, k_all, v_all, gi, r, batch, seq)
            os_.append(o)
            lses.append(lse)
        x2 = _out_proj(x2, a_n, os_, lses, b_out_norm, w_out, l)
        x2 = _ffn(x2, ffn_norm, w_gate, w_up, w_down, l)
    return x2.reshape(batch, seq, d_model)
```

```python
import functools
import math

import jax
import jax.numpy as jnp
from jax import lax
from jax.experimental import pallas as pl
from jax.experimental.pallas import tpu as pltpu

D_MODEL = 2048
HEAD_DIM = 128
CHUNK = 128
QB = 128
N_A_GROUPS = 8
A_WIDTH = N_A_GROUPS * HEAD_DIM
N_B_HEADS = 8
DILATED_CONFIGS = ((128, 1), (512, 4), (2048, 16))
DILATIONS = tuple(r for _, r in DILATED_CONFIGS)
N_B_GROUPS = len(DILATED_CONFIGS)
B_WIDTH = N_B_HEADS * HEAD_DIM
MIX_WIDTH = A_WIDTH + B_WIDTH
IN_WIDTH = 2 * A_WIDTH + N_B_GROUPS * B_WIDTH + 2 * B_WIDTH
D_FF = -(-8 * D_MODEL // (3 * 256)) * 256
ROPE_THETA = 10000.0
EPS = 1e-6

IN_TILE_N = 1024
IN_N_BLOCKS = IN_WIDTH // IN_TILE_N
IN_GELU_BLOCKS = 2 * A_WIDTH // IN_TILE_N
IN_Q_BLOCK0 = IN_GELU_BLOCKS
IN_K_BLOCK = IN_Q_BLOCK0 + N_B_GROUPS
IN_V_BLOCK = IN_K_BLOCK + 1
IN_PAIR_STEPS = 2 * IN_N_BLOCKS
SUB_N = 256
HEADS_PER_SUB = SUB_N // HEAD_DIM

TILE_M = 512
EPI_ROWS = 64
EPI_SLOTS = 4
GATE_ROWS = 1024
FFN_TILE_M = 1024
FFN_TILE_F = 512
FFN_VMEM_LIMIT = (4 * FFN_TILE_M * D_MODEL * 4 + FFN_TILE_M * D_MODEL * 2
                  + 6 * D_MODEL * FFN_TILE_F * 2 + FFN_TILE_M * FFN_TILE_F * 14)
MAX_ATT_ROWS = 4 * QB
OUT_SUB_N = 256
OUT_SLOTS = 2
VMEM_LIMIT = 52 * 1024 * 1024

F32 = jnp.float32
BF16 = jnp.bfloat16
SQRT_HALF = float(math.sqrt(0.5))
LOG2_E = float(math.log2(math.e))
LN_2 = float(math.log(2.0))
NEG_INF = float("-inf")


def _rms_normalize(x, gain):
    ms = jnp.mean(x * x, axis=-1, keepdims=True)
    return x * lax.rsqrt(ms + EPS) * gain


def _gelu_exact(x):
    return 0.5 * x * (1.0 + lax.erf(x * SQRT_HALF))


def _head(h):
    return slice(h * HEAD_DIM, (h + 1) * HEAD_DIM)


def _in_proj_kernel(x_ref, g_ref, w_ref, cos_ref, sin_ref, qg_ref, kg_ref,
                    uv_ref, q_ref, k_ref, v_ref, h_ref, acc_ref, s_ref, t_ref,
                    inv_ref, y_ref, r_ref, hn_ref, *, pos_blocks):
    t = pl.program_id(0)
    in_pair = t % IN_PAIR_STEPS
    half = t % 2
    e = jnp.maximum(t - 1, 0)
    epi_tile = 2 * (e // IN_PAIR_STEPS) + e % 2
    pos0 = (epi_tile % pos_blocks) * TILE_M

    def rope_rows(ref, rows):
        return ref[pl.ds(pl.multiple_of(pos0 + rows.start, EPI_ROWS), EPI_ROWS), :]

    def normalize_rows():
        h_ref[half] = _rms_normalize(x_ref[...], g_ref[...]).astype(BF16)

    def normalize_part(j, parts):
        rows = slice(j * (TILE_M // parts), (j + 1) * (TILE_M // parts))
        hn_ref[rows, :] = _rms_normalize(x_ref[rows, :], g_ref[...]).astype(BF16)

    def multiply(j):
        res = jnp.dot(h_ref[half], w_ref[:, j * SUB_N:(j + 1) * SUB_N],
                      preferred_element_type=F32)
        for hh in range(HEADS_PER_SUB):
            acc_ref[j * HEADS_PER_SUB + hh] = res[:, _head(hh)]

    def chunks():
        return [slice(c, c + EPI_ROWS) for c in range(0, TILE_M, EPI_ROWS)]

    def gelu_epilogue(h):
        for rows in chunks():
            uv_ref[rows, _head(h)] = _gelu_exact(acc_ref[h, rows, :]).astype(BF16)

    def split4(src, h, dst_f32, dst_bf16):
        rows = TILE_M // 4
        for c in range(4):
            piece = src[h, pl.ds(c, rows, stride=4), :]
            if dst_f32 is not None:
                dst_f32[h, c * rows:(c + 1) * rows, :] = piece
            if dst_bf16 is not None:
                dst_bf16[c * rows:(c + 1) * rows, _head(h)] = piece.astype(BF16)

    def split16_from4(h, dst_bf16):
        rows4 = TILE_M // 4
        rows16 = TILE_M // 16
        for c_lo in range(4):
            for c_hi in range(4):
                c = 4 * c_hi + c_lo
                piece = t_ref[h, pl.ds(c_lo * rows4 + c_hi, rows16, stride=4), :]
                dst_bf16[c * rows16:(c + 1) * rows16, _head(h)] = piece.astype(BF16)

    def residue_layouts(src, h, dsts):
        need16 = 16 in dsts
        if 4 in dsts or need16:
            split4(src, h, t_ref if need16 else None, dsts.get(4))
        if need16:
            split16_from4(h, dsts[16])

    def qk_epilogue(gain, dsts, h):
        strided = any(r > 1 for r in dsts)
        w = h % EPI_SLOTS
        for rows in chunks():
            a = acc_ref[h, rows, :]
            ms = jnp.mean(a * a, axis=-1, keepdims=True)
            inv_ref[w, rows, :] = jnp.broadcast_to(lax.rsqrt(ms + EPS), (EPI_ROWS, HEAD_DIM))
        for rows in chunks():
            y_ref[w, rows, :] = acc_ref[h, rows, :] * inv_ref[w, rows, :] * gain
        for rows in chunks():
            r_ref[w, rows, :] = pltpu.roll(y_ref[w, rows, :], HEAD_DIM // 2, 1)
        for rows in chunks():
            y = (y_ref[w, rows, :] * rope_rows(cos_ref, rows)
                 + r_ref[w, rows, :] * rope_rows(sin_ref, rows))
            if 1 in dsts:
                dsts[1][rows, _head(h)] = y.astype(BF16)
            if strided:
                s_ref[h, rows, :] = y
        residue_layouts(s_ref, h, dsts)

    def v_epilogue(h):
        dsts = {r: v_ref.at[gi] for gi, r in enumerate(DILATIONS)}
        for rows in chunks():
            dsts[1][rows, _head(h)] = acc_ref[h, rows, :].astype(BF16)
        residue_layouts(acc_ref, h, dsts)

    q_gain = qg_ref[...] * (HEAD_DIM ** -0.5 * LOG2_E)
    epilogues = {}
    for b in range(IN_GELU_BLOCKS):
        epilogues[b] = gelu_epilogue
    for gi, r in enumerate(DILATIONS):
        epilogues[IN_Q_BLOCK0 + gi] = functools.partial(qk_epilogue, q_gain, {r: q_ref})
    epilogues[IN_K_BLOCK] = functools.partial(
        qk_epilogue, kg_ref[...], {r: k_ref.at[gi] for gi, r in enumerate(DILATIONS)})
    epilogues[IN_V_BLOCK] = v_epilogue

    pl.when(t < 2)(normalize_rows)

    @pl.when(t == 0)
    def _():
        for j in range(IN_TILE_N // SUB_N):
            multiply(j)

    prev_block = ((t + IN_PAIR_STEPS - 1) % IN_PAIR_STEPS) // 2
    last_multiply = in_pair >= IN_PAIR_STEPS - 2
    n_sub = IN_TILE_N // SUB_N
    for b in range(IN_N_BLOCKS):
        for renew in ((False, True) if b >= IN_K_BLOCK else (False,)):
            @pl.when((prev_block == b) & (t > 0) & (last_multiply == renew))
            def _(b=b, renew=renew):
                for j in range(n_sub):
                    for hh in range(HEADS_PER_SUB):
                        epilogues[b](j * HEADS_PER_SUB + hh)
                    multiply(j)
                    if renew:
                        normalize_part(j, n_sub)
                if renew:
                    h_ref[half] = hn_ref[...]


def _in_proj(x2, gain, w_all, layer, cos_full, sin_signed, q_gain, k_gain, seq):
    m = x2.shape[0]
    row_tiles = m // TILE_M
    assert row_tiles % 2 == 0
    pos_blocks = seq // TILE_M
    ps = IN_PAIR_STEPS
    steps = (row_tiles // 2) * ps + 1

    def x_tile(t):
        nxt = 2 * (t // ps + 1) + (t % ps == ps - 1).astype(jnp.int32)
        return jnp.minimum(jnp.where(t < 2, t, nxt), row_tiles - 1)

    def epi(t):
        e = jnp.maximum(t - 1, 0)
        return 2 * (e // ps), e % ps

    def uv_index(t):
        base, r = epi(t)
        r = jnp.minimum(r, 2 * IN_GELU_BLOCKS - 1)
        return base + r % 2, r // 2

    def q_index(t):
        base, r = epi(t)
        r = jnp.clip(r - 2 * IN_Q_BLOCK0, 0, 2 * N_B_GROUPS - 1)
        return r // 2, base + r % 2, 0

    def kv_index(blk):
        def index(t):
            base, r = epi(t)
            return 0, base + (r >= 2 * blk + 1).astype(jnp.int32), 0
        return index

    return pl.pallas_call(
        functools.partial(_in_proj_kernel, pos_blocks=pos_blocks),
        grid=(steps,),
        in_specs=[
            pl.BlockSpec((TILE_M, D_MODEL), lambda t: (x_tile(t), 0)),
            pl.BlockSpec((None, 1, D_MODEL), lambda t: (layer, 0, 0)),
            pl.BlockSpec((None, D_MODEL, IN_TILE_N), lambda t: (layer, 0, (t % ps) // 2)),
            pl.BlockSpec((seq, HEAD_DIM), lambda t: (0, 0), pipeline_mode=pl.Buffered(1)),
            pl.BlockSpec((seq, HEAD_DIM), lambda t: (0, 0), pipeline_mode=pl.Buffered(1)),
            pl.BlockSpec((None, 1, HEAD_DIM), lambda t: (layer, 0, 0)),
            pl.BlockSpec((None, 1, HEAD_DIM), lambda t: (layer, 0, 0)),
        ],
        out_specs=[
            pl.BlockSpec((TILE_M, IN_TILE_N), uv_index),
            pl.BlockSpec((None, TILE_M, B_WIDTH), q_index),
            pl.BlockSpec((N_B_GROUPS, TILE_M, B_WIDTH), kv_index(IN_K_BLOCK)),
            pl.BlockSpec((N_B_GROUPS, TILE_M, B_WIDTH), kv_index(IN_V_BLOCK)),
        ],
        out_shape=[
            jax.ShapeDtypeStruct((m, 2 * A_WIDTH), BF16),
            jax.ShapeDtypeStruct((N_B_GROUPS, m, B_WIDTH), BF16),
            jax.ShapeDtypeStruct((N_B_GROUPS, m, B_WIDTH), BF16),
            jax.ShapeDtypeStruct((N_B_GROUPS, m, B_WIDTH), BF16),
        ],
        scratch_shapes=[
            pltpu.VMEM((2, TILE_M, D_MODEL), BF16),
            pltpu.VMEM((N_B_HEADS, TILE_M, HEAD_DIM), F32),
            pltpu.VMEM((N_B_HEADS, TILE_M, HEAD_DIM), F32),
            pltpu.VMEM((N_B_HEADS, TILE_M, HEAD_DIM), F32),
            pltpu.VMEM((EPI_SLOTS, TILE_M, HEAD_DIM), F32),
            pltpu.VMEM((EPI_SLOTS, TILE_M, HEAD_DIM), F32),
            pltpu.VMEM((EPI_SLOTS, TILE_M, HEAD_DIM), F32),
            pltpu.VMEM((TILE_M, D_MODEL), BF16),
        ],
        compiler_params=pltpu.CompilerParams(
            dimension_semantics=("arbitrary",),
            vmem_limit_bytes=VMEM_LIMIT),
        name="in_proj",
    )(x2, gain, w_all, cos_full, sin_signed, q_gain, k_gain)


def _gating_kernel(z_ref, lng_ref, lnb_ref, ws_ref, bs_ref, og_ref, o_ref,
                   wm_ref, vn_ref, a_ref, mu_ref, is_ref, mx_ref, rs_ref):
    row = lax.broadcasted_iota(jnp.int32, (CHUNK, CHUNK), 0)
    col = lax.broadcasted_iota(jnp.int32, (CHUNK, CHUNK), 1)
    for g in range(N_A_GROUPS):
        wm_ref[g] = jnp.where(row >= col, ws_ref[g], 0.0).astype(BF16)

    def chunk_rows(c):
        return slice(c * CHUNK, (c + 1) * CHUNK)

    def v_tile(c, g):
        return z_ref[chunk_rows(c), A_WIDTH + g * HEAD_DIM:A_WIDTH + (g + 1) * HEAD_DIM].astype(F32)

    tile = (CHUNK, HEAD_DIM)

    def ln_mean(c):
        for g in range(N_A_GROUPS):
            mu = jnp.mean(v_tile(c, g), axis=-1, keepdims=True)
            mu_ref[c % 2, g] = jnp.broadcast_to(mu, tile)

    def ln_var(c):
        for g in range(N_A_GROUPS):
            d = v_tile(c, g) - mu_ref[c % 2, g]
            var = jnp.mean(d * d, axis=-1, keepdims=True)
            is_ref[c % 2, g] = jnp.broadcast_to(lax.rsqrt(var + EPS), tile)

    def ln_apply(c):
        for g in range(N_A_GROUPS):
            vn = ((v_tile(c, g) - mu_ref[c % 2, g]) * is_ref[c % 2, g]
                  * lng_ref[g:g + 1, :] + lnb_ref[g:g + 1, :])
            vn_ref[chunk_rows(c), _head(g)] = vn.astype(BF16)

    def mix(c):
        for g in range(N_A_GROUPS):
            mx_ref[c % 2, g] = jnp.dot(wm_ref[g], vn_ref[chunk_rows(c), _head(g)],
                                       preferred_element_type=F32)

    def gate(c):
        sq = jnp.zeros(tile, F32)
        for g in range(N_A_GROUPS):
            a = z_ref[chunk_rows(c), _head(g)].astype(F32) * (mx_ref[c % 2, g] + bs_ref[g])
            sq = sq + a * a
            a_ref[chunk_rows(c), _head(g)] = a
        ms = jnp.sum(sq, axis=-1, keepdims=True) * (1.0 / A_WIDTH)
        rs_ref[c % 2] = jnp.broadcast_to(lax.rsqrt(ms + EPS), tile)

    def finish(c):
        for g in range(N_A_GROUPS):
            o_ref[chunk_rows(c), _head(g)] = (
                a_ref[chunk_rows(c), _head(g)] * rs_ref[c % 2] * og_ref[:, _head(g)]
            ).astype(BF16)

    stages = (ln_mean, ln_var, ln_apply, mix, gate, finish)
    n_chunks = GATE_ROWS // CHUNK
    for it in range(n_chunks + len(stages) - 1):
        for k in reversed(range(len(stages))):
            c = it - k
            if 0 <= c < n_chunks:
                stages[k](c)


def _gating(uv, ln_g, ln_b, w_s, b_s_b, out_gain):
    m = uv.shape[0]
    return pl.pallas_call(
        _gating_kernel,
        grid=(m // GATE_ROWS,),
        in_specs=[
            pl.BlockSpec((GATE_ROWS, 2 * A_WIDTH), lambda i: (i, 0)),
            pl.BlockSpec((N_A_GROUPS, HEAD_DIM), lambda i: (0, 0)),
            pl.BlockSpec((N_A_GROUPS, HEAD_DIM), lambda i: (0, 0)),
            pl.BlockSpec((N_A_GROUPS, CHUNK, CHUNK), lambda i: (0, 0, 0)),
            pl.BlockSpec((N_A_GROUPS, CHUNK, HEAD_DIM), lambda i: (0, 0, 0)),
            pl.BlockSpec((1, A_WIDTH), lambda i: (0, 0)),
        ],
        out_specs=pl.BlockSpec((GATE_ROWS, A_WIDTH), lambda i: (i, 0)),
        out_shape=jax.ShapeDtypeStruct((m, A_WIDTH), BF16),
        scratch_shapes=[
            pltpu.VMEM((N_A_GROUPS, CHUNK, CHUNK), BF16),
            pltpu.VMEM((GATE_ROWS, A_WIDTH), BF16),
            pltpu.VMEM((GATE_ROWS, A_WIDTH), F32),
            pltpu.VMEM((2, N_A_GROUPS, CHUNK, HEAD_DIM), F32),
            pltpu.VMEM((2, N_A_GROUPS, CHUNK, HEAD_DIM), F32),
            pltpu.VMEM((2, N_A_GROUPS, CHUNK, HEAD_DIM), F32),
            pltpu.VMEM((2, CHUNK, HEAD_DIM), F32),
        ],
        compiler_params=pltpu.CompilerParams(
            dimension_semantics=("parallel",), vmem_limit_bytes=VMEM_LIMIT),
        name="gating",
    )(uv, ln_g, ln_b, w_s, b_s_b, out_gain)


def _dilated_kernel(q_ref, k_ref, v_ref, o_ref, lse_ref, kk_ref, ve_ref,
                    sc_ref, m_ref, p_ref, *, piece_rows, att_rows):
    s = pl.program_id(2)
    n_pieces = att_rows // piece_rows
    win = QB + att_rows

    @pl.when(s == 0)
    def _():
        kk_ref[0:QB, :] = jnp.zeros((QB, B_WIDTH), BF16)
        ve_ref[:, 0:QB, 0:HEAD_DIM] = jnp.zeros((N_B_HEADS, QB, HEAD_DIM), BF16)
        ve_ref[:, :, HEAD_DIM:] = jnp.ones((N_B_HEADS, win, HEAD_DIM), BF16)

    @pl.when(s > 0)
    def _():
        kk_ref[0:QB, :] = kk_ref[att_rows:win, :]
        ve_ref[:, 0:QB, 0:HEAD_DIM] = ve_ref[:, att_rows:win, 0:HEAD_DIM]

    for p in range(n_pieces):
        rows = slice(QB + p * piece_rows, QB + (p + 1) * piece_rows)
        kk_ref[rows, :] = k_ref[p]
        for h in range(N_B_HEADS):
            ve_ref[h, rows, 0:HEAD_DIM] = v_ref[p, :, _head(h)]

    qi = lax.broadcasted_iota(jnp.int32, (QB, 2 * QB), 0)
    kj = lax.broadcasted_iota(jnp.int32, (QB, 2 * QB), 1)
    band = (kj >= qi) & (kj <= qi + QB)
    bias_band = jnp.where(band, 0.0, NEG_INF)
    bias_first = jnp.where(band & (kj >= QB), 0.0, NEG_INF)
    lane = lax.broadcasted_iota(jnp.int32, (QB, HEAD_DIM), 1)
    dn = (((1,), (1,)), ((), ()))

    def block_rows(ref, j, cols):
        if piece_rows >= QB:
            per = piece_rows // QB
            return ref[j // per, (j % per) * QB:(j % per + 1) * QB, cols]
        per = QB // piece_rows
        return jnp.concatenate([ref[j * per + t, :, cols] for t in range(per)], axis=0)

    def store_rows(ref, j, cols, val):
        if piece_rows >= QB:
            per = piece_rows // QB
            ref[j // per, (j % per) * QB:(j % per + 1) * QB, cols] = val
            return
        per = QB // piece_rows
        for t in range(per):
            ref[j * per + t, :, cols] = val[t * piece_rows:(t + 1) * piece_rows]

    units = [(j, h) for j in range(att_rows // QB) for h in range(N_B_HEADS)]

    def scores(u):
        j, h = units[u]
        q = block_rows(q_ref, j, _head(h))
        sc = lax.dot_general(q, kk_ref[j * QB:j * QB + 2 * QB, _head(h)], dn,
                             preferred_element_type=F32)
        bias = jnp.where(s == 0, bias_first, bias_band) if j == 0 else bias_band
        sc_ref[u] = sc + bias

    def row_max(u):
        m = jnp.max(sc_ref[u], axis=-1, keepdims=True)
        m_ref[u] = jnp.broadcast_to(m, (QB, HEAD_DIM))

    def probs(u):
        m = m_ref[u]
        for half in range(2):
            p_ref[u, :, _head(half)] = jnp.exp2(sc_ref[u, :, _head(half)] - m).astype(BF16)

    def weighted_values(u):
        j, h = units[u]
        oe = jnp.dot(p_ref[u], ve_ref[h, j * QB:j * QB + 2 * QB, :],
                     preferred_element_type=F32)
        den = oe[:, HEAD_DIM:]
        store_rows(o_ref, j, _head(h), (oe[:, :HEAD_DIM] / den).astype(BF16))
        m_ref[u] = m_ref[u] * LN_2 + jnp.log(den)

    stages = (scores, row_max, probs, weighted_values)
    for it in range(-(len(stages) - 1), len(units)):
        for depth, stage in enumerate(stages):
            u = it + len(stages) - 1 - depth
            if 0 <= u < len(units):
                stage(u)

    for j in range(att_rows // QB):
        tile = jnp.zeros((QB, HEAD_DIM), F32)
        for h in range(N_B_HEADS):
            tile = jnp.where(lane == h, m_ref[j * N_B_HEADS + h], tile)
        store_rows(lse_ref, j, slice(None), tile)


def _dilated_group(q_all, k_all, v_all, gi, dilation, batch, seq):
    r = dilation
    att_rows = min(MAX_ATT_ROWS, seq // r)
    units = (att_rows // QB) * N_B_HEADS
    piece_rows = min(TILE_M // r, att_rows)
    n_pieces = att_rows // piece_rows
    tiles = seq // (r * piece_rows)
    steps = tiles // n_pieces
    view = lambda a: a.reshape(N_B_GROUPS, batch, tiles, r, piece_rows, B_WIDTH)
    in_spec = pl.BlockSpec((None, None, n_pieces, None, piece_rows, B_WIDTH),
                           lambda b, c, s: (gi, b, s, c, 0, 0))
    o, lse = pl.pallas_call(
        functools.partial(_dilated_kernel, piece_rows=piece_rows, att_rows=att_rows),
        grid=(batch, r, steps),
        in_specs=[in_spec, in_spec, in_spec],
        out_specs=[
            pl.BlockSpec((None, n_pieces, None, piece_rows, B_WIDTH),
                         lambda b, c, s: (b, s, c, 0, 0)),
            pl.BlockSpec((None, n_pieces, None, piece_rows, HEAD_DIM),
                         lambda b, c, s: (b, s, c, 0, 0)),
        ],
        out_shape=[
            jax.ShapeDtypeStruct((batch, tiles, r, piece_rows, B_WIDTH), BF16),
            jax.ShapeDtypeStruct((batch, tiles, r, piece_rows, HEAD_DIM), F32),
        ],
        scratch_shapes=[
            pltpu.VMEM((QB + att_rows, B_WIDTH), BF16),
            pltpu.VMEM((N_B_HEADS, QB + att_rows, 2 * HEAD_DIM), BF16),
            pltpu.VMEM((units, QB, 2 * QB), F32),
            pltpu.VMEM((units, QB, HEAD_DIM), F32),
            pltpu.VMEM((units, QB, 2 * QB), BF16),
        ],
        compiler_params=pltpu.CompilerParams(
            dimension_semantics=("parallel", "parallel", "arbitrary"),
            vmem_limit_bytes=VMEM_LIMIT),
        name=f"dilated_r{r}",
    )(view(q_all), view(k_all), view(v_all))
    return (o.reshape(batch * seq, B_WIDTH), lse.reshape(batch * seq, HEAD_DIM))


def _out_proj_kernel(x_ref, a_ref, o0_ref, o1_ref, o2_ref, l0_ref, l1_ref, l2_ref,
                     bg_ref, w_ref, out_ref, mix_ref, b_ref, sq_ref, rs_ref, al_ref,
                     sp_ref, lslab_ref, oslab_ref):
    t = pl.program_id(0)
    o_refs = (o0_ref, o1_ref, o2_ref)
    l_refs = (l0_ref, l1_ref, l2_ref)
    n_sub = D_MODEL // OUT_SUB_N
    tile = (TILE_M, HEAD_DIM)

    def project(src, j):
        cols = slice(j * OUT_SUB_N, (j + 1) * OUT_SUB_N)
        out_ref[:, cols] = x_ref[:, cols] + jnp.dot(
            mix_ref[src], w_ref[:, cols], preferred_element_type=F32)

    def to_position_order(dst, src, r, cols):
        rows = TILE_M // r
        for c in range(r):
            dst[pl.ds(c, rows, stride=r), :] = src[c * rows:(c + 1) * rows, cols].astype(F32)

    def group_weights():
        lses = []
        for gi, r in enumerate(DILATIONS):
            if r == 1:
                lses.append(l_refs[gi][...])
            else:
                to_position_order(lslab_ref.at[gi], l_refs[gi], r, slice(None))
                lses.append(lslab_ref[gi])
        mx = jnp.maximum(jnp.maximum(lses[0], lses[1]), lses[2])
        es = [jnp.exp(l - mx) for l in lses]
        den = es[0] + es[1] + es[2]
        for gi, e in enumerate(es):
            al = e / den
            hi = al.astype(BF16)
            al_ref[gi, :, :HEAD_DIM] = hi
            al_ref[gi, :, HEAD_DIM:] = (al - hi.astype(F32)).astype(BF16)

    src_lane = lax.broadcasted_iota(jnp.int32, (2 * HEAD_DIM, 2 * HEAD_DIM), 0) % HEAD_DIM
    dst_head = lax.broadcasted_iota(jnp.int32, (2 * HEAD_DIM, 2 * HEAD_DIM), 1) // HEAD_DIM

    def spread_weights(h0):
        spread = jnp.where(src_lane == h0 + dst_head, 1.0, 0.0).astype(BF16)
        for gi in range(N_B_GROUPS):
            sp_ref[gi] = jnp.dot(al_ref[gi], spread, preferred_element_type=F32)

    def merge_head(h):
        b = None
        for gi, r in enumerate(DILATIONS):
            alpha = sp_ref[gi, :, _head(h % 2)]
            if r == 1:
                o = o_refs[gi][:, _head(h)].astype(F32)
            else:
                to_position_order(oslab_ref.at[gi, h % OUT_SLOTS], o_refs[gi], r, _head(h))
                o = oslab_ref[gi, h % OUT_SLOTS]
            b = alpha * o if b is None else b + alpha * o
        b_ref[:, _head(h)] = b
        sq_ref[...] = b * b if h == 0 else sq_ref[...] + b * b

    def inverse_rms():
        ms = jnp.sum(sq_ref[...], axis=-1, keepdims=True) * (1.0 / B_WIDTH)
        rs_ref[...] = jnp.broadcast_to(lax.rsqrt(ms + EPS), tile)

    def normalize(dst, h):
        cols = slice(A_WIDTH + h * HEAD_DIM, A_WIDTH + (h + 1) * HEAD_DIM)
        mix_ref[dst, :, cols] = (b_ref[:, _head(h)] * rs_ref[...] * bg_ref[:, _head(h)]).astype(BF16)

    def step(src, dst):
        subs = iter(range(n_sub))
        proj = (lambda: project(src, next(subs))) if src is not None else (lambda: None)
        proj()
        group_weights()
        mix_ref[dst, :, :A_WIDTH] = a_ref[...]
        proj()
        for h in range(N_B_HEADS):
            if h % 2 == 0:
                spread_weights(h)
            merge_head(h)
            if h % 2 == 1:
                proj()
        inverse_rms()
        for h in range(N_B_HEADS):
            normalize(dst, h)
            if h % 4 == 3:
                proj()

    pl.when(t == 0)(lambda: step(None, 0))
    for par in range(2):
        pl.when((t > 0) & (t % 2 == par))(functools.partial(step, 1 - par, par))


def _out_proj(x2, a_n, os_, lses, b_gain, w_all, layer):
    m = x2.shape[0]
    tiles = m // TILE_M
    done = lambda t: (jnp.maximum(t - 1, 0), 0)
    nxt = lambda t: (jnp.minimum(t, tiles - 1), 0)
    fixed = lambda t: (layer, 0, 0)
    return pl.pallas_call(
        _out_proj_kernel,
        grid=(tiles + 1,),
        in_specs=[
            pl.BlockSpec((TILE_M, D_MODEL), done),
            pl.BlockSpec((TILE_M, A_WIDTH), nxt),
            pl.BlockSpec((TILE_M, B_WIDTH), nxt),
            pl.BlockSpec((TILE_M, B_WIDTH), nxt),
            pl.BlockSpec((TILE_M, B_WIDTH), nxt),
            pl.BlockSpec((TILE_M, HEAD_DIM), nxt),
            pl.BlockSpec((TILE_M, HEAD_DIM), nxt),
            pl.BlockSpec((TILE_M, HEAD_DIM), nxt),
            pl.BlockSpec((None, 1, B_WIDTH), fixed),
            pl.BlockSpec((None, MIX_WIDTH, D_MODEL), fixed, pipeline_mode=pl.Buffered(1)),
        ],
        out_specs=pl.BlockSpec((TILE_M, D_MODEL), done),
        out_shape=jax.ShapeDtypeStruct((m, D_MODEL), F32),
        scratch_shapes=[
            pltpu.VMEM((2, TILE_M, MIX_WIDTH), BF16),
            pltpu.VMEM((TILE_M, B_WIDTH), F32),
            pltpu.VMEM((TILE_M, HEAD_DIM), F32),
            pltpu.VMEM((TILE_M, HEAD_DIM), F32),
            pltpu.VMEM((N_B_GROUPS, TILE_M, 2 * HEAD_DIM), BF16),
            pltpu.VMEM((N_B_GROUPS, TILE_M, 2 * HEAD_DIM), F32),
            pltpu.VMEM((N_B_GROUPS, TILE_M, HEAD_DIM), F32),
            pltpu.VMEM((N_B_GROUPS, OUT_SLOTS, TILE_M, HEAD_DIM), F32),
        ],
        compiler_params=pltpu.CompilerParams(
            dimension_semantics=("arbitrary",), vmem_limit_bytes=VMEM_LIMIT),
        name="out_proj",
    )(x2, a_n, *os_, *lses, b_gain, w_all)


def _ffn_kernel(x_ref, g_ref, wg_ref, wu_ref, wd_ref, out_ref, h_ref):
    f = pl.program_id(1)

    @pl.when(f == 0)
    def _():
        x = x_ref[...]
        h_ref[...] = _rms_normalize(x, g_ref[...]).astype(BF16)
        out_ref[...] = x

    h = h_ref[...]
    gate = jnp.dot(h, wg_ref[...], preferred_element_type=F32)
    up = jnp.dot(h, wu_ref[...], preferred_element_type=F32)
    act = (gate * jax.nn.sigmoid(gate) * up).astype(BF16)
    out_ref[...] += jnp.dot(act, wd_ref[...], preferred_element_type=F32)


def _ffn(x2, gain, w_gate, w_up, w_down, layer):
    m = x2.shape[0]
    return pl.pallas_call(
        _ffn_kernel,
        grid=(m // FFN_TILE_M, D_FF // FFN_TILE_F),
        in_specs=[
            pl.BlockSpec((FFN_TILE_M, D_MODEL), lambda i, f: (i, 0)),
            pl.BlockSpec((None, 1, D_MODEL), lambda i, f: (layer, 0, 0)),
            pl.BlockSpec((None, D_MODEL, FFN_TILE_F), lambda i, f: (layer, 0, f)),
            pl.BlockSpec((None, D_MODEL, FFN_TILE_F), lambda i, f: (layer, 0, f)),
            pl.BlockSpec((None, FFN_TILE_F, D_MODEL), lambda i, f: (layer, f, 0)),
        ],
        out_specs=pl.BlockSpec((FFN_TILE_M, D_MODEL), lambda i, f: (i, 0)),
        out_shape=jax.ShapeDtypeStruct((m, D_MODEL), F32),
        scratch_shapes=[pltpu.VMEM((FFN_TILE_M, D_MODEL), BF16)],
        compiler_params=pltpu.CompilerParams(
            dimension_semantics=("parallel", "arbitrary"),
            vmem_limit_bytes=FFN_VMEM_LIMIT),
        name="ffn",
    )(x2, gain, w_gate, w_up, w_down)


def _rope_tables(seq):
    pos = jnp.arange(seq, dtype=F32)
    inv_freq = 1.0 / (ROPE_THETA ** (jnp.arange(0, HEAD_DIM, 2, dtype=F32) / HEAD_DIM))
    ang = pos[:, None] * inv_freq[None, :]
    cos, sin = jnp.cos(ang), jnp.sin(ang)
    return (jnp.concatenate([cos, cos], axis=-1),
            jnp.concatenate([-sin, sin], axis=-1))


def kernel(x, mix_norm, w_in, a_ln_g, a_ln_b, a_w_s, a_b_s, q_norm, k_norm,
           a_out_norm, b_out_norm, w_out, ffn_norm, w_gate, w_up, w_down):
    batch, seq, d_model = x.shape
    depth = w_in.shape[0]
    assert d_model == D_MODEL and seq % TILE_M == 0
    assert all(w == r * QB for w, r in DILATED_CONFIGS)
    assert all(TILE_M % (r * 16) == 0 and seq % (r * 2 * QB) == 0 for r in DILATIONS)
    cos_full, sin_signed = _rope_tables(seq)
    x2 = x.reshape(batch * seq, d_model)
    w_in, w_out, w_gate, w_up, w_down = (
        w.astype(BF16) for w in (w_in, w_out, w_gate, w_up, w_down))
    as_rows = lambda p: p.reshape(depth, 1, p.shape[-1])
    mix_norm, q_norm, k_norm, b_out_norm, ffn_norm = (
        as_rows(p) for p in (mix_norm, q_norm, k_norm, b_out_norm, ffn_norm))
    for l in range(depth):
        uv, q_all, k_all, v_all = _in_proj(
            x2, mix_norm, w_in, l, cos_full, sin_signed, q_norm, k_norm, seq)
        b_s_b = jnp.broadcast_to(a_b_s[l][:, :, None], (N_A_GROUPS, CHUNK, HEAD_DIM))
        a_n = _gating(uv, a_ln_g[l], a_ln_b[l], a_w_s[l], b_s_b, a_out_norm[l][None])
        os_, lses = [], []
        for gi, r in enumerate(DILATIONS):
            o, lse = _dilated_group(q_all, k_all, v_all, gi, r, batch, seq)
            os_.append(o)
            lses.append(lse)
        x2 = _out_proj(x2, a_n, os_, lses, b_out_norm, w_out, l)
        x2 = _ffn(x2, ffn_norm, w_gate, w_up, w_down, l)
    return x2.reshape(batch, seq, d_model)
```

```python
import functools
import math

import jax
import jax.numpy as jnp
from jax import lax
from jax.experimental import pallas as pl
from jax.experimental.pallas import tpu as pltpu

D_MODEL = 2048
HEAD_DIM = 128
CHUNK = 128
QB = 128
N_A_GROUPS = 8
A_WIDTH = N_A_GROUPS * HEAD_DIM
N_B_HEADS = 8
DILATED_CONFIGS = ((128, 1), (512, 4), (2048, 16))
DILATIONS = tuple(r for _, r in DILATED_CONFIGS)
N_B_GROUPS = len(DILATED_CONFIGS)
B_WIDTH = N_B_HEADS * HEAD_DIM
MIX_WIDTH = A_WIDTH + B_WIDTH
IN_WIDTH = 2 * A_WIDTH + N_B_GROUPS * B_WIDTH + 2 * B_WIDTH
D_FF = -(-8 * D_MODEL // (3 * 256)) * 256
ROPE_THETA = 10000.0
EPS = 1e-6

IN_TILE_N = 1024
IN_N_BLOCKS = IN_WIDTH // IN_TILE_N
IN_GELU_BLOCKS = 2 * A_WIDTH // IN_TILE_N
IN_Q_BLOCK0 = IN_GELU_BLOCKS
IN_K_BLOCK = IN_Q_BLOCK0 + N_B_GROUPS
IN_V_BLOCK = IN_K_BLOCK + 1
IN_PAIR_STEPS = 2 * IN_N_BLOCKS
SUB_N = 256
HEADS_PER_SUB = SUB_N // HEAD_DIM

TILE_M = 512
EPI_ROWS = 64
EPI_SLOTS = 4
GATE_ROWS = 2048
FFN_TILE_M = 1024
FFN_TILE_F = 512
FFN_VMEM_LIMIT = (4 * FFN_TILE_M * D_MODEL * 4 + FFN_TILE_M * D_MODEL * 2
                  + 6 * D_MODEL * FFN_TILE_F * 2 + FFN_TILE_M * FFN_TILE_F * 14)
MAX_ATT_ROWS = 8 * QB
OUT_SUB_N = 256
OUT_SLOTS = 2
VMEM_LIMIT = 52 * 1024 * 1024

F32 = jnp.float32
BF16 = jnp.bfloat16
SQRT_HALF = float(math.sqrt(0.5))
LOG2_E = float(math.log2(math.e))
LN_2 = float(math.log(2.0))
NEG_INF = float("-inf")


def _rms_normalize(x, gain):
    ms = jnp.mean(x * x, axis=-1, keepdims=True)
    return x * lax.rsqrt(ms + EPS) * gain


def _gelu_exact(x):
    return 0.5 * x * (1.0 + lax.erf(x * SQRT_HALF))


def _head(h):
    return slice(h * HEAD_DIM, (h + 1) * HEAD_DIM)


def _in_proj_kernel(x_ref, g_ref, w_ref, cos_ref, sin_ref, qg_ref, kg_ref,
                    uv_ref, q_ref, k_ref, v_ref, h_ref, acc_ref, s_ref, t_ref,
                    inv_ref, y_ref, r_ref, hn_ref, *, pos_blocks):
    t = pl.program_id(0)
    in_pair = t % IN_PAIR_STEPS
    half = t % 2
    e = jnp.maximum(t - 1, 0)
    epi_tile = 2 * (e // IN_PAIR_STEPS) + e % 2
    pos0 = (epi_tile % pos_blocks) * TILE_M

    def rope_rows(ref, rows):
        return ref[pl.ds(pl.multiple_of(pos0 + rows.start, EPI_ROWS), EPI_ROWS), :]

    def normalize_rows():
        h_ref[half] = _rms_normalize(x_ref[...], g_ref[...]).astype(BF16)

    def normalize_part(j, parts):
        rows = slice(j * (TILE_M // parts), (j + 1) * (TILE_M // parts))
        hn_ref[rows, :] = _rms_normalize(x_ref[rows, :], g_ref[...]).astype(BF16)

    def multiply(j):
        res = jnp.dot(h_ref[half], w_ref[:, j * SUB_N:(j + 1) * SUB_N],
                      preferred_element_type=F32)
        for hh in range(HEADS_PER_SUB):
            acc_ref[j * HEADS_PER_SUB + hh] = res[:, _head(hh)]

    def chunks():
        return [slice(c, c + EPI_ROWS) for c in range(0, TILE_M, EPI_ROWS)]

    def gelu_epilogue(h):
        for rows in chunks():
            uv_ref[rows, _head(h)] = _gelu_exact(acc_ref[h, rows, :]).astype(BF16)

    def split4(src, h, dst_f32, dst_bf16):
        rows = TILE_M // 4
        for c in range(4):
            piece = src[h, pl.ds(c, rows, stride=4), :]
            if dst_f32 is not None:
                dst_f32[h, c * rows:(c + 1) * rows, :] = piece
            if dst_bf16 is not None:
                dst_bf16[c * rows:(c + 1) * rows, _head(h)] = piece.astype(BF16)

    def split16_from4(h, dst_bf16):
        rows4 = TILE_M // 4
        rows16 = TILE_M // 16
        for c_lo in range(4):
            for c_hi in range(4):
                c = 4 * c_hi + c_lo
                piece = t_ref[h, pl.ds(c_lo * rows4 + c_hi, rows16, stride=4), :]
                dst_bf16[c * rows16:(c + 1) * rows16, _head(h)] = piece.astype(BF16)

    def residue_layouts(src, h, dsts):
        need16 = 16 in dsts
        if 4 in dsts or need16:
            split4(src, h, t_ref if need16 else None, dsts.get(4))
        if need16:
            split16_from4(h, dsts[16])

    def qk_epilogue(gain, dsts, h):
        strided = any(r > 1 for r in dsts)
        w = h % EPI_SLOTS
        for rows in chunks():
            a = acc_ref[h, rows, :]
            ms = jnp.mean(a * a, axis=-1, keepdims=True)
            inv_ref[w, rows, :] = jnp.broadcast_to(lax.rsqrt(ms + EPS), (EPI_ROWS, HEAD_DIM))
        for rows in chunks():
            y_ref[w, rows, :] = acc_ref[h, rows, :] * inv_ref[w, rows, :] * gain
        for rows in chunks():
            r_ref[w, rows, :] = pltpu.roll(y_ref[w, rows, :], HEAD_DIM // 2, 1)
        for rows in chunks():
            y = (y_ref[w, rows, :] * rope_rows(cos_ref, rows)
                 + r_ref[w, rows, :] * rope_rows(sin_ref, rows))
            if 1 in dsts:
                dsts[1][rows, _head(h)] = y.astype(BF16)
            if strided:
                s_ref[h, rows, :] = y
        residue_layouts(s_ref, h, dsts)

    def v_epilogue(h):
        dsts = {r: v_ref.at[gi] for gi, r in enumerate(DILATIONS)}
        for rows in chunks():
            dsts[1][rows, _head(h)] = acc_ref[h, rows, :].astype(BF16)
        residue_layouts(acc_ref, h, dsts)

    q_gain = qg_ref[...] * (HEAD_DIM ** -0.5 * LOG2_E)
    epilogues = {}
    for b in range(IN_GELU_BLOCKS):
        epilogues[b] = gelu_epilogue
    for gi, r in enumerate(DILATIONS):
        epilogues[IN_Q_BLOCK0 + gi] = functools.partial(qk_epilogue, q_gain, {r: q_ref})
    epilogues[IN_K_BLOCK] = functools.partial(
        qk_epilogue, kg_ref[...], {r: k_ref.at[gi] for gi, r in enumerate(DILATIONS)})
    epilogues[IN_V_BLOCK] = v_epilogue

    pl.when(t < 2)(normalize_rows)

    @pl.when(t == 0)
    def _():
        for j in range(IN_TILE_N // SUB_N):
            multiply(j)

    prev_block = ((t + IN_PAIR_STEPS - 1) % IN_PAIR_STEPS) // 2
    last_multiply = in_pair >= IN_PAIR_STEPS - 2
    n_sub = IN_TILE_N // SUB_N
    for b in range(IN_N_BLOCKS):
        for renew in ((False, True) if b >= IN_K_BLOCK else (False,)):
            @pl.when((prev_block == b) & (t > 0) & (last_multiply == renew))
            def _(b=b, renew=renew):
                for j in range(n_sub):
                    for hh in range(HEADS_PER_SUB):
                        epilogues[b](j * HEADS_PER_SUB + hh)
                    multiply(j)
                    if renew:
                        normalize_part(j, n_sub)
                if renew:
                    h_ref[half] = hn_ref[...]


def _in_proj(x2, gain, w_all, layer, cos_full, sin_signed, q_gain, k_gain, seq):
    m = x2.shape[0]
    row_tiles = m // TILE_M
    assert row_tiles % 2 == 0
    pos_blocks = seq // TILE_M
    ps = IN_PAIR_STEPS
    steps = (row_tiles // 2) * ps + 1

    def x_tile(t):
        nxt = 2 * (t // ps + 1) + (t % ps == ps - 1).astype(jnp.int32)
        return jnp.minimum(jnp.where(t < 2, t, nxt), row_tiles - 1)

    def epi(t):
        e = jnp.maximum(t - 1, 0)
        return 2 * (e // ps), e % ps

    def uv_index(t):
        base, r = epi(t)
        r = jnp.minimum(r, 2 * IN_GELU_BLOCKS - 1)
        return base + r % 2, r // 2

    def q_index(t):
        base, r = epi(t)
        r = jnp.clip(r - 2 * IN_Q_BLOCK0, 0, 2 * N_B_GROUPS - 1)
        return r // 2, base + r % 2, 0

    def kv_index(blk):
        def index(t):
            base, r = epi(t)
            return 0, base + (r >= 2 * blk + 1).astype(jnp.int32), 0
        return index

    return pl.pallas_call(
        functools.partial(_in_proj_kernel, pos_blocks=pos_blocks),
        grid=(steps,),
        in_specs=[
            pl.BlockSpec((TILE_M, D_MODEL), lambda t: (x_tile(t), 0)),
            pl.BlockSpec((None, 1, D_MODEL), lambda t: (layer, 0, 0)),
            pl.BlockSpec((None, D_MODEL, IN_TILE_N), lambda t: (layer, 0, (t % ps) // 2)),
            pl.BlockSpec((seq, HEAD_DIM), lambda t: (0, 0), pipeline_mode=pl.Buffered(1)),
            pl.BlockSpec((seq, HEAD_DIM), lambda t: (0, 0), pipeline_mode=pl.Buffered(1)),
            pl.BlockSpec((None, 1, HEAD_DIM), lambda t: (layer, 0, 0)),
            pl.BlockSpec((None, 1, HEAD_DIM), lambda t: (layer, 0, 0)),
        ],
        out_specs=[
            pl.BlockSpec((TILE_M, IN_TILE_N), uv_index),
            pl.BlockSpec((None, TILE_M, B_WIDTH), q_index),
            pl.BlockSpec((N_B_GROUPS, TILE_M, B_WIDTH), kv_index(IN_K_BLOCK)),
            pl.BlockSpec((N_B_GROUPS, TILE_M, B_WIDTH), kv_index(IN_V_BLOCK)),
        ],
        out_shape=[
            jax.ShapeDtypeStruct((m, 2 * A_WIDTH), BF16),
            jax.ShapeDtypeStruct((N_B_GROUPS, m, B_WIDTH), BF16),
            jax.ShapeDtypeStruct((N_B_GROUPS, m, B_WIDTH), BF16),
            jax.ShapeDtypeStruct((N_B_GROUPS, m, B_WIDTH), BF16),
        ],
        scratch_shapes=[
            pltpu.VMEM((2, TILE_M, D_MODEL), BF16),
            pltpu.VMEM((N_B_HEADS, TILE_M, HEAD_DIM), F32),
            pltpu.VMEM((N_B_HEADS, TILE_M, HEAD_DIM), F32),
            pltpu.VMEM((N_B_HEADS, TILE_M, HEAD_DIM), F32),
            pltpu.VMEM((EPI_SLOTS, TILE_M, HEAD_DIM), F32),
            pltpu.VMEM((EPI_SLOTS, TILE_M, HEAD_DIM), F32),
            pltpu.VMEM((EPI_SLOTS, TILE_M, HEAD_DIM), F32),
            pltpu.VMEM((TILE_M, D_MODEL), BF16),
        ],
        compiler_params=pltpu.CompilerParams(
            dimension_semantics=("arbitrary",),
            vmem_limit_bytes=VMEM_LIMIT),
        name="in_proj",
    )(x2, gain, w_all, cos_full, sin_signed, q_gain, k_gain)


def _gating_kernel(z_ref, lng_ref, lnb_ref, ws_ref, bs_ref, og_ref, o_ref,
                   wm_ref, vn_ref, a_ref, mu_ref, is_ref, mx_ref, rs_ref):
    row = lax.broadcasted_iota(jnp.int32, (CHUNK, CHUNK), 0)
    col = lax.broadcasted_iota(jnp.int32, (CHUNK, CHUNK), 1)
    for g in range(N_A_GROUPS):
        wm_ref[g] = jnp.where(row >= col, ws_ref[g], 0.0).astype(BF16)

    def chunk_rows(c):
        return slice(c * CHUNK, (c + 1) * CHUNK)

    def v_tile(c, g):
        return z_ref[chunk_rows(c), A_WIDTH + g * HEAD_DIM:A_WIDTH + (g + 1) * HEAD_DIM].astype(F32)

    tile = (CHUNK, HEAD_DIM)

    def ln_mean(c):
        for g in range(N_A_GROUPS):
            mu = jnp.mean(v_tile(c, g), axis=-1, keepdims=True)
            mu_ref[c % 2, g] = jnp.broadcast_to(mu, tile)

    def ln_var(c):
        for g in range(N_A_GROUPS):
            d = v_tile(c, g) - mu_ref[c % 2, g]
            var = jnp.mean(d * d, axis=-1, keepdims=True)
            is_ref[c % 2, g] = jnp.broadcast_to(lax.rsqrt(var + EPS), tile)

    def ln_apply(c):
        for g in range(N_A_GROUPS):
            vn = ((v_tile(c, g) - mu_ref[c % 2, g]) * is_ref[c % 2, g]
                  * lng_ref[g:g + 1, :] + lnb_ref[g:g + 1, :])
            vn_ref[chunk_rows(c), _head(g)] = vn.astype(BF16)

    def mix(c):
        for g in range(N_A_GROUPS):
            mx_ref[c % 2, g] = jnp.dot(wm_ref[g], vn_ref[chunk_rows(c), _head(g)],
                                       preferred_element_type=F32)

    def gate(c):
        sq = jnp.zeros(tile, F32)
        for g in range(N_A_GROUPS):
            a = z_ref[chunk_rows(c), _head(g)].astype(F32) * (mx_ref[c % 2, g] + bs_ref[g])
            sq = sq + a * a
            a_ref[chunk_rows(c), _head(g)] = a
        ms = jnp.sum(sq, axis=-1, keepdims=True) * (1.0 / A_WIDTH)
        rs_ref[c % 2] = jnp.broadcast_to(lax.rsqrt(ms + EPS), tile)

    def finish(c):
        for g in range(N_A_GROUPS):
            o_ref[chunk_rows(c), _head(g)] = (
                a_ref[chunk_rows(c), _head(g)] * rs_ref[c % 2] * og_ref[:, _head(g)]
            ).astype(BF16)

    stages = (ln_mean, ln_var, ln_apply, mix, gate, finish)
    n_chunks = GATE_ROWS // CHUNK
    for it in range(n_chunks + len(stages) - 1):
        for k in reversed(range(len(stages))):
            c = it - k
            if 0 <= c < n_chunks:
                stages[k](c)


def _gating(uv, ln_g, ln_b, w_s, b_s_b, out_gain):
    m = uv.shape[0]
    return pl.pallas_call(
        _gating_kernel,
        grid=(m // GATE_ROWS,),
        in_specs=[
            pl.BlockSpec((GATE_ROWS, 2 * A_WIDTH), lambda i: (i, 0)),
            pl.BlockSpec((N_A_GROUPS, HEAD_DIM), lambda i: (0, 0)),
            pl.BlockSpec((N_A_GROUPS, HEAD_DIM), lambda i: (0, 0)),
            pl.BlockSpec((N_A_GROUPS, CHUNK, CHUNK), lambda i: (0, 0, 0)),
            pl.BlockSpec((N_A_GROUPS, CHUNK, HEAD_DIM), lambda i: (0, 0, 0)),
            pl.BlockSpec((1, A_WIDTH), lambda i: (0, 0)),
        ],
        out_specs=pl.BlockSpec((GATE_ROWS, A_WIDTH), lambda i: (i, 0)),
        out_shape=jax.ShapeDtypeStruct((m, A_WIDTH), BF16),
        scratch_shapes=[
            pltpu.VMEM((N_A_GROUPS, CHUNK, CHUNK), BF16),
            pltpu.VMEM((GATE_ROWS, A_WIDTH), BF16),
            pltpu.VMEM((GATE_ROWS, A_WIDTH), F32),
            pltpu.VMEM((2, N_A_GROUPS, CHUNK, HEAD_DIM), F32),
            pltpu.VMEM((2, N_A_GROUPS, CHUNK, HEAD_DIM), F32),
            pltpu.VMEM((2, N_A_GROUPS, CHUNK, HEAD_DIM), F32),
            pltpu.VMEM((2, CHUNK, HEAD_DIM), F32),
        ],
        compiler_params=pltpu.CompilerParams(
            dimension_semantics=("parallel",), vmem_limit_bytes=VMEM_LIMIT),
        name="gating",
    )(uv, ln_g, ln_b, w_s, b_s_b, out_gain)


def _dilated_kernel(q_ref, k_ref, v_ref, o_ref, lse_ref, kk_ref, ve_ref,
                    sc_ref, m_ref, p_ref, *, piece_rows, att_rows):
    s = pl.program_id(2)
    n_pieces = att_rows // piece_rows
    win = QB + att_rows

    @pl.when(s == 0)
    def _():
        kk_ref[0:QB, :] = jnp.zeros((QB, B_WIDTH), BF16)
        ve_ref[:, 0:QB, 0:HEAD_DIM] = jnp.zeros((N_B_HEADS, QB, HEAD_DIM), BF16)
        ve_ref[:, :, HEAD_DIM:] = jnp.ones((N_B_HEADS, win, HEAD_DIM), BF16)

    @pl.when(s > 0)
    def _():
        kk_ref[0:QB, :] = kk_ref[att_rows:win, :]
        ve_ref[:, 0:QB, 0:HEAD_DIM] = ve_ref[:, att_rows:win, 0:HEAD_DIM]

    for p in range(n_pieces):
        rows = slice(QB + p * piece_rows, QB + (p + 1) * piece_rows)
        kk_ref[rows, :] = k_ref[p]
        for h in range(N_B_HEADS):
            ve_ref[h, rows, 0:HEAD_DIM] = v_ref[p, :, _head(h)]

    qi = lax.broadcasted_iota(jnp.int32, (QB, 2 * QB), 0)
    kj = lax.broadcasted_iota(jnp.int32, (QB, 2 * QB), 1)
    band = (kj >= qi) & (kj <= qi + QB)
    bias_band = jnp.where(band, 0.0, NEG_INF)
    bias_first = jnp.where(band & (kj >= QB), 0.0, NEG_INF)
    lane = lax.broadcasted_iota(jnp.int32, (QB, HEAD_DIM), 1)
    dn = (((1,), (1,)), ((), ()))

    def block_rows(ref, j, cols):
        if piece_rows >= QB:
            per = piece_rows // QB
            return ref[j // per, (j % per) * QB:(j % per + 1) * QB, cols]
        per = QB // piece_rows
        return jnp.concatenate([ref[j * per + t, :, cols] for t in range(per)], axis=0)

    def store_rows(ref, j, cols, val):
        if piece_rows >= QB:
            per = piece_rows // QB
            ref[j // per, (j % per) * QB:(j % per + 1) * QB, cols] = val
            return
        per = QB // piece_rows
        for t in range(per):
            ref[j * per + t, :, cols] = val[t * piece_rows:(t + 1) * piece_rows]

    units = [(j, h) for j in range(att_rows // QB) for h in range(N_B_HEADS)]

    def scores(u):
        j, h = units[u]
        q = block_rows(q_ref, j, _head(h))
        sc = lax.dot_general(q, kk_ref[j * QB:j * QB + 2 * QB, _head(h)], dn,
                             preferred_element_type=F32)
        bias = jnp.where(s == 0, bias_first, bias_band) if j == 0 else bias_band
        sc_ref[u] = sc + bias

    def row_max(u):
        m = jnp.max(sc_ref[u], axis=-1, keepdims=True)
        m_ref[u] = jnp.broadcast_to(m, (QB, HEAD_DIM))

    def probs(u):
        m = m_ref[u]
        for half in range(2):
            p_ref[u, :, _head(half)] = jnp.exp2(sc_ref[u, :, _head(half)] - m).astype(BF16)

    def weighted_values(u):
        j, h = units[u]
        oe = jnp.dot(p_ref[u], ve_ref[h, j * QB:j * QB + 2 * QB, :],
                     preferred_element_type=F32)
        den = oe[:, HEAD_DIM:]
        store_rows(o_ref, j, _head(h), (oe[:, :HEAD_DIM] / den).astype(BF16))
        m_ref[u] = m_ref[u] * LN_2 + jnp.log(den)

    stages = (scores, row_max, probs, weighted_values)
    for it in range(-(len(stages) - 1), len(units)):
        for depth, stage in enumerate(stages):
            u = it + len(stages) - 1 - depth
            if 0 <= u < len(units):
                stage(u)

    for j in range(att_rows // QB):
        tile = jnp.zeros((QB, HEAD_DIM), F32)
        for h in range(N_B_HEADS):
            tile = jnp.where(lane == h, m_ref[j * N_B_HEADS + h], tile)
        store_rows(lse_ref, j, slice(None), tile)


def _dilated_group(q_all, k_all, v_all, gi, dilation, batch, seq):
    r = dilation
    att_rows = min(MAX_ATT_ROWS, seq // r)
    units = (att_rows // QB) * N_B_HEADS
    piece_rows = min(TILE_M // r, att_rows)
    n_pieces = att_rows // piece_rows
    tiles = seq // (r * piece_rows)
    steps = tiles // n_pieces
    view = lambda a: a.reshape(N_B_GROUPS, batch, tiles, r, piece_rows, B_WIDTH)
    in_spec = pl.BlockSpec((None, None, n_pieces, None, piece_rows, B_WIDTH),
                           lambda b, c, s: (gi, b, s, c, 0, 0))
    o, lse = pl.pallas_call(
        functools.partial(_dilated_kernel, piece_rows=piece_rows, att_rows=att_rows),
        grid=(batch, r, steps),
        in_specs=[in_spec, in_spec, in_spec],
        out_specs=[
            pl.BlockSpec((None, n_pieces, None, piece_rows, B_WIDTH),
                         lambda b, c, s: (b, s, c, 0, 0)),
            pl.BlockSpec((None, n_pieces, None, piece_rows, HEAD_DIM),
                         lambda b, c, s: (b, s, c, 0, 0)),
        ],
        out_shape=[
            jax.ShapeDtypeStruct((batch, tiles, r, piece_rows, B_WIDTH), BF16),
            jax.ShapeDtypeStruct((batch, tiles, r, piece_rows, HEAD_DIM), F32),
        ],
        scratch_shapes=[
            pltpu.VMEM((QB + att_rows, B_WIDTH), BF16),
            pltpu.VMEM((N_B_HEADS, QB + att_rows, 2 * HEAD_DIM), BF16),
            pltpu.VMEM((units, QB, 2 * QB), F32),
            pltpu.VMEM((units, QB, HEAD_DIM), F32),
            pltpu.VMEM((units, QB, 2 * QB), BF16),
        ],
        compiler_params=pltpu.CompilerParams(
            dimension_semantics=("parallel", "parallel", "arbitrary"),
            vmem_limit_bytes=VMEM_LIMIT),
        name=f"dilated_r{r}",
    )(view(q_all), view(k_all), view(v_all))
    return (o.reshape(batch * seq, B_WIDTH), lse.reshape(batch * seq, HEAD_DIM))


def _out_proj_kernel(x_ref, a_ref, o0_ref, o1_ref, o2_ref, l0_ref, l1_ref, l2_ref,
                     bg_ref, w_ref, out_ref, mix_ref, b_ref, sq_ref, rs_ref, al_ref,
                     sp_ref, lslab_ref, oslab_ref):
    t = pl.program_id(0)
    o_refs = (o0_ref, o1_ref, o2_ref)
    l_refs = (l0_ref, l1_ref, l2_ref)
    n_sub = D_MODEL // OUT_SUB_N
    tile = (TILE_M, HEAD_DIM)

    def project(src, j):
        cols = slice(j * OUT_SUB_N, (j + 1) * OUT_SUB_N)
        out_ref[:, cols] = x_ref[:, cols] + jnp.dot(
            mix_ref[src], w_ref[:, cols], preferred_element_type=F32)

    def to_position_order(dst, src, r, cols):
        rows = TILE_M // r
        for c in range(r):
            dst[pl.ds(c, rows, stride=r), :] = src[c * rows:(c + 1) * rows, cols].astype(F32)

    def group_weights():
        lses = []
        for gi, r in enumerate(DILATIONS):
            if r == 1:
                lses.append(l_refs[gi][...])
            else:
                to_position_order(lslab_ref.at[gi], l_refs[gi], r, slice(None))
                lses.append(lslab_ref[gi])
        mx = jnp.maximum(jnp.maximum(lses[0], lses[1]), lses[2])
        es = [jnp.exp(l - mx) for l in lses]
        den = es[0] + es[1] + es[2]
        for gi, e in enumerate(es):
            al = e / den
            hi = al.astype(BF16)
            al_ref[gi, :, :HEAD_DIM] = hi
            al_ref[gi, :, HEAD_DIM:] = (al - hi.astype(F32)).astype(BF16)

    src_lane = lax.broadcasted_iota(jnp.int32, (2 * HEAD_DIM, 2 * HEAD_DIM), 0) % HEAD_DIM
    dst_head = lax.broadcasted_iota(jnp.int32, (2 * HEAD_DIM, 2 * HEAD_DIM), 1) // HEAD_DIM

    def spread_weights(h0):
        spread = jnp.where(src_lane == h0 + dst_head, 1.0, 0.0).astype(BF16)
        for gi in range(N_B_GROUPS):
            sp_ref[gi] = jnp.dot(al_ref[gi], spread, preferred_element_type=F32)

    def merge_head(h):
        b = None
        for gi, r in enumerate(DILATIONS):
            alpha = sp_ref[gi, :, _head(h % 2)]
            if r == 1:
                o = o_refs[gi][:, _head(h)].astype(F32)
            else:
                to_position_order(oslab_ref.at[gi, h % OUT_SLOTS], o_refs[gi], r, _head(h))
                o = oslab_ref[gi, h % OUT_SLOTS]
            b = alpha * o if b is None else b + alpha * o
        b_ref[:, _head(h)] = b
        sq_ref[...] = b * b if h == 0 else sq_ref[...] + b * b

    def inverse_rms():
        ms = jnp.sum(sq_ref[...], axis=-1, keepdims=True) * (1.0 / B_WIDTH)
        rs_ref[...] = jnp.broadcast_to(lax.rsqrt(ms + EPS), tile)

    def normalize(dst, h):
        cols = slice(A_WIDTH + h * HEAD_DIM, A_WIDTH + (h + 1) * HEAD_DIM)
        mix_ref[dst, :, cols] = (b_ref[:, _head(h)] * rs_ref[...] * bg_ref[:, _head(h)]).astype(BF16)

    def step(src, dst):
        subs = iter(range(n_sub))
        proj = (lambda: project(src, next(subs))) if src is not None else (lambda: None)
        proj()
        group_weights()
        mix_ref[dst, :, :A_WIDTH] = a_ref[...]
        proj()
        for h in range(N_B_HEADS):
            if h % 2 == 0:
                spread_weights(h)
            merge_head(h)
            if h % 2 == 1:
                proj()
        inverse_rms()
        for h in range(N_B_HEADS):
            normalize(dst, h)
            if h % 4 == 3:
                proj()

    pl.when(t == 0)(lambda: step(None, 0))
    for par in range(2):
        pl.when((t > 0) & (t % 2 == par))(functools.partial(step, 1 - par, par))


def _out_proj(x2, a_n, os_, lses, b_gain, w_all, layer):
    m = x2.shape[0]
    tiles = m // TILE_M
    done = lambda t: (jnp.maximum(t - 1, 0), 0)
    nxt = lambda t: (jnp.minimum(t, tiles - 1), 0)
    fixed = lambda t: (layer, 0, 0)
    return pl.pallas_call(
        _out_proj_kernel,
        grid=(tiles + 1,),
        in_specs=[
            pl.BlockSpec((TILE_M, D_MODEL), done),
            pl.BlockSpec((TILE_M, A_WIDTH), nxt),
            pl.BlockSpec((TILE_M, B_WIDTH), nxt),
            pl.BlockSpec((TILE_M, B_WIDTH), nxt),
            pl.BlockSpec((TILE_M, B_WIDTH), nxt),
            pl.BlockSpec((TILE_M, HEAD_DIM), nxt),
            pl.BlockSpec((TILE_M, HEAD_DIM), nxt),
            pl.BlockSpec((TILE_M, HEAD_DIM), nxt),
            pl.BlockSpec((None, 1, B_WIDTH), fixed),
            pl.BlockSpec((None, MIX_WIDTH, D_MODEL), fixed, pipeline_mode=pl.Buffered(1)),
        ],
        out_specs=pl.BlockSpec((TILE_M, D_MODEL), done),
        out_shape=jax.ShapeDtypeStruct((m, D_MODEL), F32),
        scratch_shapes=[
            pltpu.VMEM((2, TILE_M, MIX_WIDTH), BF16),
            pltpu.VMEM((TILE_M, B_WIDTH), F32),
            pltpu.VMEM((TILE_M, HEAD_DIM), F32),
            pltpu.VMEM((TILE_M, HEAD_DIM), F32),
            pltpu.VMEM((N_B_GROUPS, TILE_M, 2 * HEAD_DIM), BF16),
            pltpu.VMEM((N_B_GROUPS, TILE_M, 2 * HEAD_DIM), F32),
            pltpu.VMEM((N_B_GROUPS, TILE_M, HEAD_DIM), F32),
            pltpu.VMEM((N_B_GROUPS, OUT_SLOTS, TILE_M, HEAD_DIM), F32),
        ],
        compiler_params=pltpu.CompilerParams(
            dimension_semantics=("arbitrary",), vmem_limit_bytes=VMEM_LIMIT),
        name="out_proj",
    )(x2, a_n, *os_, *lses, b_gain, w_all)


def _ffn_kernel(x_ref, g_ref, wg_ref, wu_ref, wd_ref, out_ref, h_ref):
    f = pl.program_id(1)

    @pl.when(f == 0)
    def _():
        x = x_ref[...]
        h_ref[...] = _rms_normalize(x, g_ref[...]).astype(BF16)
        out_ref[...] = x

    h = h_ref[...]
    gate = jnp.dot(h, wg_ref[...], preferred_element_type=F32)
    up = jnp.dot(h, wu_ref[...], preferred_element_type=F32)
    act = (gate * jax.nn.sigmoid(gate) * up).astype(BF16)
    out_ref[...] += jnp.dot(act, wd_ref[...], preferred_element_type=F32)


def _ffn(x2, gain, w_gate, w_up, w_down, layer):
    m = x2.shape[0]
    return pl.pallas_call(
        _ffn_kernel,
        grid=(m // FFN_TILE_M, D_FF // FFN_TILE_F),
        in_specs=[
            pl.BlockSpec((FFN_TILE_M, D_MODEL), lambda i, f: (i, 0)),
            pl.BlockSpec((None, 1, D_MODEL), lambda i, f: (layer, 0, 0)),
            pl.BlockSpec((None, D_MODEL, FFN_TILE_F), lambda i, f: (layer, 0, f)),
            pl.BlockSpec((None, D_MODEL, FFN_TILE_F), lambda i, f: (layer, 0, f)),
            pl.BlockSpec((None, FFN_TILE_F, D_MODEL), lambda i, f: (layer, f, 0)),
        ],
        out_specs=pl.BlockSpec((FFN_TILE_M, D_MODEL), lambda i, f: (i, 0)),
        out_shape=jax.ShapeDtypeStruct((m, D_MODEL), F32),
        scratch_shapes=[pltpu.VMEM((FFN_TILE_M, D_MODEL), BF16)],
        compiler_params=pltpu.CompilerParams(
            dimension_semantics=("parallel", "arbitrary"),
            vmem_limit_bytes=FFN_VMEM_LIMIT),
        name="ffn",
    )(x2, gain, w_gate, w_up, w_down)


def _rope_tables(seq):
    pos = jnp.arange(seq, dtype=F32)
    inv_freq = 1.0 / (ROPE_THETA ** (jnp.arange(0, HEAD_DIM, 2, dtype=F32) / HEAD_DIM))
    ang = pos[:, None] * inv_freq[None, :]
    cos, sin = jnp.cos(ang), jnp.sin(ang)
    return (jnp.concatenate([cos, cos], axis=-1),
            jnp.concatenate([-sin, sin], axis=-1))


def kernel(x, mix_norm, w_in, a_ln_g, a_ln_b, a_w_s, a_b_s, q_norm, k_norm,
           a_out_norm, b_out_norm, w_out, ffn_norm, w_gate, w_up, w_down):
    batch, seq, d_model = x.shape
    depth = w_in.shape[0]
    assert d_model == D_MODEL and seq % TILE_M == 0
    assert all(w == r * QB for w, r in DILATED_CONFIGS)
    assert all(TILE_M % (r * 16) == 0 and seq % (r * 2 * QB) == 0 for r in DILATIONS)
    cos_full, sin_signed = _rope_tables(seq)
    x2 = x.reshape(batch * seq, d_model)
    w_in, w_out, w_gate, w_up, w_down = (
        w.astype(BF16) for w in (w_in, w_out, w_gate, w_up, w_down))
    as_rows = lambda p: p.reshape(depth, 1, p.shape[-1])
    mix_norm, q_norm, k_norm, b_out_norm, ffn_norm = (
        as_rows(p) for p in (mix_norm, q_norm, k_norm, b_out_norm, ffn_norm))
    for l in range(depth):
        uv, q_all, k_all, v_all = _in_proj(
            x2, mix_norm, w_in, l, cos_full, sin_signed, q_norm, k_norm, seq)
        b_s_b = jnp.broadcast_to(a_b_s[l][:, :, None], (N_A_GROUPS, CHUNK, HEAD_DIM))
        a_n = _gating(uv, a_ln_g[l], a_ln_b[l], a_w_s[l], b_s_b, a_out_norm[l][None])
        os_, lses = [], []
        for gi, r in enumerate(DILATIONS):
            o, lse = _dilated_group(q_all, k_all, v_all, gi, r, batch, seq)
            os_.append(o)
            lses.append(lse)
        x2 = _out_proj(x2, a_n, os_, lses, b_out_norm, w_out, l)
        x2 = _ffn(x2, ffn_norm, w_gate, w_up, w_down, l)
    return x2.reshape(batch, seq, d_model)
```

```python
import functools
import math

import jax
import jax.numpy as jnp
from jax import lax
from jax.experimental import pallas as pl
from jax.experimental.pallas import tpu as pltpu

D_MODEL = 2048
HEAD_DIM = 128
CHUNK = 128
QB = 128
N_A_GROUPS = 8
A_WIDTH = N_A_GROUPS * HEAD_DIM
N_B_HEADS = 8
DILATED_CONFIGS = ((128, 1), (512, 4), (2048, 16))
DILATIONS = tuple(r for _, r in DILATED_CONFIGS)
N_B_GROUPS = len(DILATED_CONFIGS)
B_WIDTH = N_B_HEADS * HEAD_DIM
MIX_WIDTH = A_WIDTH + B_WIDTH
IN_WIDTH = 2 * A_WIDTH + N_B_GROUPS * B_WIDTH + 2 * B_WIDTH
D_FF = -(-8 * D_MODEL // (3 * 256)) * 256
ROPE_THETA = 10000.0
EPS = 1e-6

IN_TILE_N = 1024
IN_N_BLOCKS = IN_WIDTH // IN_TILE_N
IN_GELU_BLOCKS = 2 * A_WIDTH // IN_TILE_N
IN_Q_BLOCK0 = IN_GELU_BLOCKS
IN_K_BLOCK = IN_Q_BLOCK0 + N_B_GROUPS
IN_V_BLOCK = IN_K_BLOCK + 1
IN_PAIR_STEPS = 2 * IN_N_BLOCKS
SUB_N = 256
HEADS_PER_SUB = SUB_N // HEAD_DIM

TILE_M = 512
EPI_ROWS = 64
EPI_SLOTS = 4
GATE_ROWS = 2048
FFN_TILE_M = 1024
FFN_TILE_F = 512
FFN_VMEM_LIMIT = (4 * FFN_TILE_M * D_MODEL * 4 + FFN_TILE_M * D_MODEL * 2
                  + 6 * D_MODEL * FFN_TILE_F * 2 + FFN_TILE_M * FFN_TILE_F * 14)
MAX_ATT_ROWS = 8 * QB
OUT_SUB_N = 256
OUT_SLOTS = 2
VMEM_LIMIT = 52 * 1024 * 1024

F32 = jnp.float32
BF16 = jnp.bfloat16
SQRT_HALF = float(math.sqrt(0.5))
LOG2_E = float(math.log2(math.e))
LN_2 = float(math.log(2.0))
NEG_INF = float("-inf")


def _rms_normalize(x, gain):
    ms = jnp.mean(x * x, axis=-1, keepdims=True)
    return x * lax.rsqrt(ms + EPS) * gain


def _gelu_exact(x):
    return 0.5 * x * (1.0 + lax.erf(x * SQRT_HALF))


def _head(h):
    return slice(h * HEAD_DIM, (h + 1) * HEAD_DIM)


def _in_proj_kernel(x_ref, g_ref, w_ref, cos_ref, sin_ref, qg_ref, kg_ref,
                    uv_ref, q_ref, k_ref, v_ref, h_ref, acc_ref, s_ref, t_ref,
                    inv_ref, y_ref, r_ref, hn_ref, *, pos_blocks):
    t = pl.program_id(0)
    in_pair = t % IN_PAIR_STEPS
    half = t % 2
    e = jnp.maximum(t - 1, 0)
    epi_tile = 2 * (e // IN_PAIR_STEPS) + e % 2
    pos0 = (epi_tile % pos_blocks) * TILE_M

    def rope_rows(ref, rows):
        return ref[pl.ds(pl.multiple_of(pos0 + rows.start, EPI_ROWS), EPI_ROWS), :]

    def normalize_rows():
        h_ref[half] = _rms_normalize(x_ref[...], g_ref[...]).astype(BF16)

    def normalize_part(j, parts):
        rows = slice(j * (TILE_M // parts), (j + 1) * (TILE_M // parts))
        hn_ref[rows, :] = _rms_normalize(x_ref[rows, :], g_ref[...]).astype(BF16)

    def multiply(j):
        res = jnp.dot(h_ref[half], w_ref[:, j * SUB_N:(j + 1) * SUB_N],
                      preferred_element_type=F32)
        for hh in range(HEADS_PER_SUB):
            acc_ref[j * HEADS_PER_SUB + hh] = res[:, _head(hh)]

    def chunks():
        return [slice(c, c + EPI_ROWS) for c in range(0, TILE_M, EPI_ROWS)]

    def gelu_epilogue(h):
        for rows in chunks():
            uv_ref[rows, _head(h)] = _gelu_exact(acc_ref[h, rows, :]).astype(BF16)

    def split4(src, h, dst_f32, dst_bf16):
        rows = TILE_M // 4
        for c in range(4):
            piece = src[h, pl.ds(c, rows, stride=4), :]
            if dst_f32 is not None:
                dst_f32[h, c * rows:(c + 1) * rows, :] = piece
            if dst_bf16 is not None:
                dst_bf16[c * rows:(c + 1) * rows, _head(h)] = piece.astype(BF16)

    def split16_from4(h, dst_bf16):
        rows4 = TILE_M // 4
        rows16 = TILE_M // 16
        for c_lo in range(4):
            for c_hi in range(4):
                c = 4 * c_hi + c_lo
                piece = t_ref[h, pl.ds(c_lo * rows4 + c_hi, rows16, stride=4), :]
                dst_bf16[c * rows16:(c + 1) * rows16, _head(h)] = piece.astype(BF16)

    def residue_layouts(src, h, dsts):
        need16 = 16 in dsts
        if 4 in dsts or need16:
            split4(src, h, t_ref if need16 else None, dsts.get(4))
        if need16:
            split16_from4(h, dsts[16])

    def qk_epilogue(gain, dsts, h):
        strided = any(r > 1 for r in dsts)
        w = h % EPI_SLOTS
        for rows in chunks():
            a = acc_ref[h, rows, :]
            ms = jnp.mean(a * a, axis=-1, keepdims=True)
            inv_ref[w, rows, :] = jnp.broadcast_to(lax.rsqrt(ms + EPS), (EPI_ROWS, HEAD_DIM))
        for rows in chunks():
            y_ref[w, rows, :] = acc_ref[h, rows, :] * inv_ref[w, rows, :] * gain
        for rows in chunks():
            r_ref[w, rows, :] = pltpu.roll(y_ref[w, rows, :], HEAD_DIM // 2, 1)
        for rows in chunks():
            y = (y_ref[w, rows, :] * rope_rows(cos_ref, rows)
                 + r_ref[w, rows, :] * rope_rows(sin_ref, rows))
            if 1 in dsts:
                dsts[1][rows, _head(h)] = y.astype(BF16)
            if strided:
                s_ref[h, rows, :] = y
        residue_layouts(s_ref, h, dsts)

    def v_epilogue(h):
        dsts = {r: v_ref.at[gi] for gi, r in enumerate(DILATIONS)}
        for rows in chunks():
            dsts[1][rows, _head(h)] = acc_ref[h, rows, :].astype(BF16)
        residue_layouts(acc_ref, h, dsts)

    q_gain = qg_ref[...] * (HEAD_DIM ** -0.5 * LOG2_E)
    epilogues = {}
    for b in range(IN_GELU_BLOCKS):
        epilogues[b] = gelu_epilogue
    for gi, r in enumerate(DILATIONS):
        epilogues[IN_Q_BLOCK0 + gi] = functools.partial(qk_epilogue, q_gain, {r: q_ref})
    epilogues[IN_K_BLOCK] = functools.partial(
        qk_epilogue, kg_ref[...], {r: k_ref.at[gi] for gi, r in enumerate(DILATIONS)})
    epilogues[IN_V_BLOCK] = v_epilogue

    pl.when(t < 2)(normalize_rows)

    @pl.when(t == 0)
    def _():
        for j in range(IN_TILE_N // SUB_N):
            multiply(j)

    prev_block = ((t + IN_PAIR_STEPS - 1) % IN_PAIR_STEPS) // 2
    last_multiply = in_pair >= IN_PAIR_STEPS - 2
    n_sub = IN_TILE_N // SUB_N
    for b in range(IN_N_BLOCKS):
        for renew in ((False, True) if b >= IN_K_BLOCK else (False,)):
            @pl.when((prev_block == b) & (t > 0) & (last_multiply == renew))
            def _(b=b, renew=renew):
                for j in range(n_sub):
                    for hh in range(HEADS_PER_SUB):
                        epilogues[b](j * HEADS_PER_SUB + hh)
                    multiply(j)
                    if renew:
                        normalize_part(j, n_sub)
                if renew:
                    h_ref[half] = hn_ref[...]


def _in_proj(x2, gain, w_all, layer, cos_full, sin_signed, q_gain, k_gain, seq):
    m = x2.shape[0]
    row_tiles = m // TILE_M
    assert row_tiles % 2 == 0
    pos_blocks = seq // TILE_M
    ps = IN_PAIR_STEPS
    steps = (row_tiles // 2) * ps + 1

    def x_tile(t):
        nxt = 2 * (t // ps + 1) + (t % ps == ps - 1).astype(jnp.int32)
        return jnp.minimum(jnp.where(t < 2, t, nxt), row_tiles - 1)

    def epi(t):
        e = jnp.maximum(t - 1, 0)
        return 2 * (e // ps), e % ps

    def uv_index(t):
        base, r = epi(t)
        r = jnp.minimum(r, 2 * IN_GELU_BLOCKS - 1)
        return base + r % 2, r // 2

    def q_index(t):
        base, r = epi(t)
        r = jnp.clip(r - 2 * IN_Q_BLOCK0, 0, 2 * N_B_GROUPS - 1)
        return r // 2, base + r % 2, 0

    def kv_index(blk):
        def index(t):
            base, r = epi(t)
            return 0, base + (r >= 2 * blk + 1).astype(jnp.int32), 0
        return index

    return pl.pallas_call(
        functools.partial(_in_proj_kernel, pos_blocks=pos_blocks),
        grid=(steps,),
        in_specs=[
            pl.BlockSpec((TILE_M, D_MODEL), lambda t: (x_tile(t), 0)),
            pl.BlockSpec((None, 1, D_MODEL), lambda t: (layer, 0, 0)),
            pl.BlockSpec((None, D_MODEL, IN_TILE_N), lambda t: (layer, 0, (t % ps) // 2)),
            pl.BlockSpec((seq, HEAD_DIM), lambda t: (0, 0), pipeline_mode=pl.Buffered(1)),
            pl.BlockSpec((seq, HEAD_DIM), lambda t: (0, 0), pipeline_mode=pl.Buffered(1)),
            pl.BlockSpec((None, 1, HEAD_DIM), lambda t: (layer, 0, 0)),
            pl.BlockSpec((None, 1, HEAD_DIM), lambda t: (layer, 0, 0)),
        ],
        out_specs=[
            pl.BlockSpec((TILE_M, IN_TILE_N), uv_index),
            pl.BlockSpec((None, TILE_M, B_WIDTH), q_index),
            pl.BlockSpec((N_B_GROUPS, TILE_M, B_WIDTH), kv_index(IN_K_BLOCK)),
            pl.BlockSpec((N_B_GROUPS, TILE_M, B_WIDTH), kv_index(IN_V_BLOCK)),
        ],
        out_shape=[
            jax.ShapeDtypeStruct((m, 2 * A_WIDTH), BF16),
            jax.ShapeDtypeStruct((N_B_GROUPS, m, B_WIDTH), BF16),
            jax.ShapeDtypeStruct((N_B_GROUPS, m, B_WIDTH), BF16),
            jax.ShapeDtypeStruct((N_B_GROUPS, m, B_WIDTH), BF16),
        ],
        scratch_shapes=[
            pltpu.VMEM((2, TILE_M, D_MODEL), BF16),
            pltpu.VMEM((N_B_HEADS, TILE_M, HEAD_DIM), F32),
            pltpu.VMEM((N_B_HEADS, TILE_M, HEAD_DIM), F32),
            pltpu.VMEM((N_B_HEADS, TILE_M, HEAD_DIM), F32),
            pltpu.VMEM((EPI_SLOTS, TILE_M, HEAD_DIM), F32),
            pltpu.VMEM((EPI_SLOTS, TILE_M, HEAD_DIM), F32),
            pltpu.VMEM((EPI_SLOTS, TILE_M, HEAD_DIM), F32),
            pltpu.VMEM((TILE_M, D_MODEL), BF16),
        ],
        compiler_params=pltpu.CompilerParams(
            dimension_semantics=("arbitrary",),
            vmem_limit_bytes=VMEM_LIMIT),
        name="in_proj",
    )(x2, gain, w_all, cos_full, sin_signed, q_gain, k_gain)


def _gating_kernel(z_ref, lng_ref, lnb_ref, ws_ref, bs_ref, og_ref, o_ref,
                   wm_ref, vn_ref, a_ref, mu_ref, is_ref, mx_ref, rs_ref):
    row = lax.broadcasted_iota(jnp.int32, (CHUNK, CHUNK), 0)
    col = lax.broadcasted_iota(jnp.int32, (CHUNK, CHUNK), 1)
    for g in range(N_A_GROUPS):
        wm_ref[g] = jnp.where(row >= col, ws_ref[g], 0.0).astype(BF16)

    def chunk_rows(c):
        return slice(c * CHUNK, (c + 1) * CHUNK)

    def v_tile(c, g):
        return z_ref[chunk_rows(c), A_WIDTH + g * HEAD_DIM:A_WIDTH + (g + 1) * HEAD_DIM].astype(F32)

    tile = (CHUNK, HEAD_DIM)

    def ln_mean(c):
        for g in range(N_A_GROUPS):
            mu = jnp.mean(v_tile(c, g), axis=-1, keepdims=True)
            mu_ref[c % 2, g] = jnp.broadcast_to(mu, tile)

    def ln_var(c):
        for g in range(N_A_GROUPS):
            d = v_tile(c, g) - mu_ref[c % 2, g]
            var = jnp.mean(d * d, axis=-1, keepdims=True)
            is_ref[c % 2, g] = jnp.broadcast_to(lax.rsqrt(var + EPS), tile)

    def ln_apply(c):
        for g in range(N_A_GROUPS):
            vn = ((v_tile(c, g) - mu_ref[c % 2, g]) * is_ref[c % 2, g]
                  * lng_ref[g:g + 1, :] + lnb_ref[g:g + 1, :])
            vn_ref[chunk_rows(c), _head(g)] = vn.astype(BF16)

    def mix(c):
        for g in range(N_A_GROUPS):
            mx_ref[c % 2, g] = jnp.dot(wm_ref[g], vn_ref[chunk_rows(c), _head(g)],
                                       preferred_element_type=F32)

    def gate(c):
        sq = jnp.zeros(tile, F32)
        for g in range(N_A_GROUPS):
            a = z_ref[chunk_rows(c), _head(g)].astype(F32) * (mx_ref[c % 2, g] + bs_ref[g])
            sq = sq + a * a
            a_ref[chunk_rows(c), _head(g)] = a
        ms = jnp.sum(sq, axis=-1, keepdims=True) * (1.0 / A_WIDTH)
        rs_ref[c % 2] = jnp.broadcast_to(lax.rsqrt(ms + EPS), tile)

    def finish(c):
        for g in range(N_A_GROUPS):
            o_ref[chunk_rows(c), _head(g)] = (
                a_ref[chunk_rows(c), _head(g)] * rs_ref[c % 2] * og_ref[:, _head(g)]
            ).astype(BF16)

    stages = (ln_mean, ln_var, ln_apply, mix, gate, finish)
    n_chunks = GATE_ROWS // CHUNK
    for it in range(n_chunks + len(stages) - 1):
        for k in reversed(range(len(stages))):
            c = it - k
            if 0 <= c < n_chunks:
                stages[k](c)


def _gating(uv, ln_g, ln_b, w_s, b_s_b, out_gain):
    m = uv.shape[0]
    return pl.pallas_call(
        _gating_kernel,
        grid=(m // GATE_ROWS,),
        in_specs=[
            pl.BlockSpec((GATE_ROWS, 2 * A_WIDTH), lambda i: (i, 0)),
            pl.BlockSpec((N_A_GROUPS, HEAD_DIM), lambda i: (0, 0)),
            pl.BlockSpec((N_A_GROUPS, HEAD_DIM), lambda i: (0, 0)),
            pl.BlockSpec((N_A_GROUPS, CHUNK, CHUNK), lambda i: (0, 0, 0)),
            pl.BlockSpec((N_A_GROUPS, CHUNK, HEAD_DIM), lambda i: (0, 0, 0)),
            pl.BlockSpec((1, A_WIDTH), lambda i: (0, 0)),
        ],
        out_specs=pl.BlockSpec((GATE_ROWS, A_WIDTH), lambda i: (i, 0)),
        out_shape=jax.ShapeDtypeStruct((m, A_WIDTH), BF16),
        scratch_shapes=[
            pltpu.VMEM((N_A_GROUPS, CHUNK, CHUNK), BF16),
            pltpu.VMEM((GATE_ROWS, A_WIDTH), BF16),
            pltpu.VMEM((GATE_ROWS, A_WIDTH), F32),
            pltpu.VMEM((2, N_A_GROUPS, CHUNK, HEAD_DIM), F32),
            pltpu.VMEM((2, N_A_GROUPS, CHUNK, HEAD_DIM), F32),
            pltpu.VMEM((2, N_A_GROUPS, CHUNK, HEAD_DIM), F32),
            pltpu.VMEM((2, CHUNK, HEAD_DIM), F32),
        ],
        compiler_params=pltpu.CompilerParams(
            dimension_semantics=("parallel",), vmem_limit_bytes=VMEM_LIMIT),
        name="gating",
    )(uv, ln_g, ln_b, w_s, b_s_b, out_gain)


def _dilated_kernel(q_ref, k_ref, v_ref, o_ref, lse_ref, kk_ref, ve_ref,
                    sc_ref, m_ref, p_ref, *, piece_rows, att_rows):
    s = pl.program_id(2)
    n_pieces = att_rows // piece_rows
    win = QB + att_rows

    @pl.when(s == 0)
    def _():
        kk_ref[0:QB, :] = jnp.zeros((QB, B_WIDTH), BF16)
        ve_ref[:, 0:QB, 0:HEAD_DIM] = jnp.zeros((N_B_HEADS, QB, HEAD_DIM), BF16)
        ve_ref[:, :, HEAD_DIM:] = jnp.ones((N_B_HEADS, win, HEAD_DIM), BF16)

    @pl.when(s > 0)
    def _():
        kk_ref[0:QB, :] = kk_ref[att_rows:win, :]
        ve_ref[:, 0:QB, 0:HEAD_DIM] = ve_ref[:, att_rows:win, 0:HEAD_DIM]

    for p in range(n_pieces):
        rows = slice(QB + p * piece_rows, QB + (p + 1) * piece_rows)
        kk_ref[rows, :] = k_ref[p]
        for h in range(N_B_HEADS):
            ve_ref[h, rows, 0:HEAD_DIM] = v_ref[p, :, _head(h)]

    qi = lax.broadcasted_iota(jnp.int32, (QB, 2 * QB), 0)
    kj = lax.broadcasted_iota(jnp.int32, (QB, 2 * QB), 1)
    band = (kj >= qi) & (kj <= qi + QB)
    bias_band = jnp.where(band, 0.0, NEG_INF)
    bias_first = jnp.where(band & (kj >= QB), 0.0, NEG_INF)
    lane = lax.broadcasted_iota(jnp.int32, (QB, HEAD_DIM), 1)
    dn = (((1,), (1,)), ((), ()))

    def block_rows(ref, j, cols):
        if piece_rows >= QB:
            per = piece_rows // QB
            return ref[j // per, (j % per) * QB:(j % per + 1) * QB, cols]
        per = QB // piece_rows
        return jnp.concatenate([ref[j * per + t, :, cols] for t in range(per)], axis=0)

    def store_rows(ref, j, cols, val):
        if piece_rows >= QB:
            per = piece_rows // QB
            ref[j // per, (j % per) * QB:(j % per + 1) * QB, cols] = val
            return
        per = QB // piece_rows
        for t in range(per):
            ref[j * per + t, :, cols] = val[t * piece_rows:(t + 1) * piece_rows]

    units = [(j, h) for j in range(att_rows // QB) for h in range(N_B_HEADS)]

    def scores(u):
        j, h = units[u]
        q = block_rows(q_ref, j, _head(h))
        sc = lax.dot_general(q, kk_ref[j * QB:j * QB + 2 * QB, _head(h)], dn,
                             preferred_element_type=F32)
        bias = jnp.where(s == 0, bias_first, bias_band) if j == 0 else bias_band
        sc_ref[u] = sc + bias

    def row_max(u):
        m = jnp.max(sc_ref[u], axis=-1, keepdims=True)
        m_ref[u] = jnp.broadcast_to(m, (QB, HEAD_DIM))

    def probs(u):
        m = m_ref[u]
        for half in range(2):
            p_ref[u, :, _head(half)] = jnp.exp2(sc_ref[u, :, _head(half)] - m).astype(BF16)

    def weighted_values(u):
        j, h = units[u]
        oe = jnp.dot(p_ref[u], ve_ref[h, j * QB:j * QB + 2 * QB, :],
                     preferred_element_type=F32)
        den = oe[:, HEAD_DIM:]
        store_rows(o_ref, j, _head(h), (oe[:, :HEAD_DIM] / den).astype(BF16))
        m_ref[u] = m_ref[u] * LN_2 + jnp.log(den)

    stages = (scores, row_max, probs, weighted_values)
    for it in range(-(len(stages) - 1), len(units)):
        for depth, stage in enumerate(stages):
            u = it + len(stages) - 1 - depth
            if 0 <= u < len(units):
                stage(u)

    for j in range(att_rows // QB):
        tile = jnp.zeros((QB, HEAD_DIM), F32)
        for h in range(N_B_HEADS):
            tile = jnp.where(lane == h, m_ref[j * N_B_HEADS + h], tile)
        store_rows(lse_ref, j, slice(None), tile)


def _dilated_group(q_all, k_all, v_all, gi, dilation, batch, seq):
    r = dilation
    att_rows = min(MAX_ATT_ROWS, seq // r)
    units = (att_rows // QB) * N_B_HEADS
    piece_rows = min(TILE_M // r, att_rows)
    n_pieces = att_rows // piece_rows
    tiles = seq // (r * piece_rows)
    steps = tiles // n_pieces
    view = lambda a: a.reshape(N_B_GROUPS, batch, tiles, r, piece_rows, B_WIDTH)
    in_spec = pl.BlockSpec((None, None, n_pieces, None, piece_rows, B_WIDTH),
                           lambda b, c, s: (gi, b, s, c, 0, 0))
    o, lse = pl.pallas_call(
        functools.partial(_dilated_kernel, piece_rows=piece_rows, att_rows=att_rows),
        grid=(batch, r, steps),
        in_specs=[in_spec, in_spec, in_spec],
        out_specs=[
            pl.BlockSpec((None, n_pieces, None, piece_rows, B_WIDTH),
                         lambda b, c, s: (b, s, c, 0, 0)),
            pl.BlockSpec((None, n_pieces, None, piece_rows, HEAD_DIM),
                         lambda b, c, s: (b, s, c, 0, 0)),
        ],
        out_shape=[
            jax.ShapeDtypeStruct((batch, tiles, r, piece_rows, B_WIDTH), BF16),
            jax.ShapeDtypeStruct((batch, tiles, r, piece_rows, HEAD_DIM), F32),
        ],
        scratch_shapes=[
            pltpu.VMEM((QB + att_rows, B_WIDTH), BF16),
            pltpu.VMEM((N_B_HEADS, QB + att_rows, 2 * HEAD_DIM), BF16),
            pltpu.VMEM((units, QB, 2 * QB), F32),
            pltpu.VMEM((units, QB, HEAD_DIM), F32),
            pltpu.VMEM((units, QB, 2 * QB), BF16),
        ],
        compiler_params=pltpu.CompilerParams(
            dimension_semantics=("parallel", "parallel", "arbitrary"),
            vmem_limit_bytes=VMEM_LIMIT),
        name=f"dilated_r{r}",
    )(view(q_all), view(k_all), view(v_all))
    return (o.reshape(batch * seq, B_WIDTH), lse.reshape(batch * seq, HEAD_DIM))


def _out_proj_kernel(x_ref, a_ref, o0_ref, o1_ref, o2_ref, l0_ref, l1_ref, l2_ref,
                     bg_ref, w_ref, out_ref, mix_ref, b_ref, sq_ref, rs_ref, al_ref,
                     sp_ref, lslab_ref, oslab_ref):
    t = pl.program_id(0)
    o_refs = (o0_ref, o1_ref, o2_ref)
    l_refs = (l0_ref, l1_ref, l2_ref)
    n_sub = D_MODEL // OUT_SUB_N
    tile = (TILE_M, HEAD_DIM)

    def project(src, j):
        cols = slice(j * OUT_SUB_N, (j + 1) * OUT_SUB_N)
        out_ref[:, cols] = x_ref[:, cols] + jnp.dot(
            mix_ref[src], w_ref[:, cols], preferred_element_type=F32)

    def to_position_order(dst, src, r, cols):
        rows = TILE_M // r
        for c in range(r):
            dst[pl.ds(c, rows, stride=r), :] = src[c * rows:(c + 1) * rows, cols].astype(F32)

    def group_weights():
        lses = []
        for gi, r in enumerate(DILATIONS):
            if r == 1:
                lses.append(l_refs[gi][...])
            else:
                to_position_order(lslab_ref.at[gi], l_refs[gi], r, slice(None))
                lses.append(lslab_ref[gi])
        mx = jnp.maximum(jnp.maximum(lses[0], lses[1]), lses[2])
        es = [jnp.exp(l - mx) for l in lses]
        den = es[0] + es[1] + es[2]
        for gi, e in enumerate(es):
            al = e / den
            hi = al.astype(BF16)
            al_ref[gi, :, :HEAD_DIM] = hi
            al_ref[gi, :, HEAD_DIM:] = (al - hi.astype(F32)).astype(BF16)

    src_lane = lax.broadcasted_iota(jnp.int32, (2 * HEAD_DIM, 2 * HEAD_DIM), 0) % HEAD_DIM
    dst_head = lax.broadcasted_iota(jnp.int32, (2 * HEAD_DIM, 2 * HEAD_DIM), 1) // HEAD_DIM

    def spread_weights(h0):
        spread = jnp.where(src_lane == h0 + dst_head, 1.0, 0.0).astype(BF16)
        for gi in range(N_B_GROUPS):
            sp_ref[gi] = jnp.dot(al_ref[gi], spread, preferred_element_type=F32)

    def merge_head(h):
        b = None
        for gi, r in enumerate(DILATIONS):
            alpha = sp_ref[gi, :, _head(h % 2)]
            if r == 1:
                o = o_refs[gi][:, _head(h)].astype(F32)
            else:
                to_position_order(oslab_ref.at[gi, h % OUT_SLOTS], o_refs[gi], r, _head(h))
                o = oslab_ref[gi, h % OUT_SLOTS]
            b = alpha * o if b is None else b + alpha * o
        b_ref[:, _head(h)] = b
        sq_ref[...] = b * b if h == 0 else sq_ref[...] + b * b

    def inverse_rms():
        ms = jnp.sum(sq_ref[...], axis=-1, keepdims=True) * (1.0 / B_WIDTH)
        rs_ref[...] = jnp.broadcast_to(lax.rsqrt(ms + EPS), tile)

    def normalize(dst, h):
        cols = slice(A_WIDTH + h * HEAD_DIM, A_WIDTH + (h + 1) * HEAD_DIM)
        mix_ref[dst, :, cols] = (b_ref[:, _head(h)] * rs_ref[...] * bg_ref[:, _head(h)]).astype(BF16)

    def step(src, dst):
        subs = iter(range(n_sub))
        proj = (lambda: project(src, next(subs))) if src is not None else (lambda: None)
        proj()
        group_weights()
        mix_ref[dst, :, :A_WIDTH] = a_ref[...]
        proj()
        for h in range(N_B_HEADS):
            if h % 2 == 0:
                spread_weights(h)
            merge_head(h)
            if h % 2 == 1:
                proj()
        inverse_rms()
        for h in range(N_B_HEADS):
            normalize(dst, h)
            if h % 4 == 3:
                proj()

    pl.when(t == 0)(lambda: step(None, 0))
    for par in range(2):
        pl.when((t > 0) & (t % 2 == par))(functools.partial(step, 1 - par, par))


def _out_proj(x2, a_n, os_, lses, b_gain, w_all, layer):
    m = x2.shape[0]
    tiles = m // TILE_M
    done = lambda t: (jnp.maximum(t - 1, 0), 0)
    nxt = lambda t: (jnp.minimum(t, tiles - 1), 0)
    fixed = lambda t: (layer, 0, 0)
    return pl.pallas_call(
        _out_proj_kernel,
        grid=(tiles + 1,),
        in_specs=[
            pl.BlockSpec((TILE_M, D_MODEL), done),
            pl.BlockSpec((TILE_M, A_WIDTH), nxt),
            pl.BlockSpec((TILE_M, B_WIDTH), nxt),
            pl.BlockSpec((TILE_M, B_WIDTH), nxt),
            pl.BlockSpec((TILE_M, B_WIDTH), nxt),
            pl.BlockSpec((TILE_M, HEAD_DIM), nxt),
            pl.BlockSpec((TILE_M, HEAD_DIM), nxt),
            pl.BlockSpec((TILE_M, HEAD_DIM), nxt),
            pl.BlockSpec((None, 1, B_WIDTH), fixed),
            pl.BlockSpec((None, MIX_WIDTH, D_MODEL), fixed, pipeline_mode=pl.Buffered(1)),
        ],
        out_specs=pl.BlockSpec((TILE_M, D_MODEL), done),
        out_shape=jax.ShapeDtypeStruct((m, D_MODEL), F32),
        scratch_shapes=[
            pltpu.VMEM((2, TILE_M, MIX_WIDTH), BF16),
            pltpu.VMEM((TILE_M, B_WIDTH), F32),
            pltpu.VMEM((TILE_M, HEAD_DIM), F32),
            pltpu.VMEM((TILE_M, HEAD_DIM), F32),
            pltpu.VMEM((N_B_GROUPS, TILE_M, 2 * HEAD_DIM), BF16),
            pltpu.VMEM((N_B_GROUPS, TILE_M, 2 * HEAD_DIM), F32),
            pltpu.VMEM((N_B_GROUPS, TILE_M, HEAD_DIM), F32),
            pltpu.VMEM((N_B_GROUPS, OUT_SLOTS, TILE_M, HEAD_DIM), F32),
        ],
        compiler_params=pltpu.CompilerParams(
            dimension_semantics=("arbitrary",), vmem_limit_bytes=VMEM_LIMIT),
        name="out_proj",
    )(x2, a_n, *os_, *lses, b_gain, w_all)


def _ffn_kernel(x_ref, g_ref, wg_ref, wu_ref, wd_ref, out_ref, h_ref):
    f = pl.program_id(1)

    @pl.when(f == 0)
    def _():
        x = x_ref[...]
        h_ref[...] = _rms_normalize(x, g_ref[...]).astype(BF16)
        out_ref[...] = x

    h = h_ref[...]
    gate = jnp.dot(h, wg_ref[...], preferred_element_type=F32)
    up = jnp.dot(h, wu_ref[...], preferred_element_type=F32)
    act = (gate * jax.nn.sigmoid(gate) * up).astype(BF16)
    out_ref[...] += jnp.dot(act, wd_ref[...], preferred_element_type=F32)


def _ffn(x2, gain, w_gate, w_up, w_down, layer):
    m = x2.shape[0]
    return pl.pallas_call(
        _ffn_kernel,
        grid=(m // FFN_TILE_M, D_FF // FFN_TILE_F),
        in_specs=[
            pl.BlockSpec((FFN_TILE_M, D_MODEL), lambda i, f: (i, 0)),
            pl.BlockSpec((None, 1, D_MODEL), lambda i, f: (layer, 0, 0)),
            pl.BlockSpec((None, D_MODEL, FFN_TILE_F), lambda i, f: (layer, 0, f)),
            pl.BlockSpec((None, D_MODEL, FFN_TILE_F), lambda i, f: (layer, 0, f)),
            pl.BlockSpec((None, FFN_TILE_F, D_MODEL), lambda i, f: (layer, f, 0)),
        ],
        out_specs=pl.BlockSpec((FFN_TILE_M, D_MODEL), lambda i, f: (i, 0)),
        out_shape=jax.ShapeDtypeStruct((m, D_MODEL), F32),
        scratch_shapes=[pltpu.VMEM((FFN_TILE_M, D_MODEL), BF16)],
        compiler_params=pltpu.CompilerParams(
            dimension_semantics=("parallel", "arbitrary"),
            vmem_limit_bytes=FFN_VMEM_LIMIT),
        name="ffn",
    )(x2, gain, w_gate, w_up, w_down)


def _rope_tables(seq):
    pos = jnp.arange(seq, dtype=F32)
    inv_freq = 1.0 / (ROPE_THETA ** (jnp.arange(0, HEAD_DIM, 2, dtype=F32) / HEAD_DIM))
    ang = pos[:, None] * inv_freq[None, :]
    cos, sin = jnp.cos(ang), jnp.sin(ang)
    return (jnp.concatenate([cos, cos], axis=-1),
            jnp.concatenate([-sin, sin], axis=-1))


def kernel(x, mix_norm, w_in, a_ln_g, a_ln_b, a_w_s, a_b_s, q_norm, k_norm,
           a_out_norm, b_out_norm, w_out, ffn_norm, w_gate, w_up, w_down):
    batch, seq, d_model = x.shape
    depth = w_in.shape[0]
    assert d_model == D_MODEL and seq % TILE_M == 0
    assert all(w == r * QB for w, r in DILATED_CONFIGS)
    assert DILATIONS == (1, 4, 16)
    assert all(TILE_M % (r * 16) == 0 and seq % (r * 2 * QB) == 0 for r in DILATIONS)
    cos_full, sin_signed = _rope_tables(seq)
    x2 = x.reshape(batch * seq, d_model)
    w_in, w_out, w_gate, w_up, w_down = (
        w.astype(BF16) for w in (w_in, w_out, w_gate, w_up, w_down))
    as_rows = lambda p: p.reshape(depth, 1, p.shape[-1])
    mix_norm, q_norm, k_norm, b_out_norm, ffn_norm = (
        as_rows(p) for p in (mix_norm, q_norm, k_norm, b_out_norm, ffn_norm))
    for l in range(depth):
        uv, q_all, k_all, v_all = _in_proj(
            x2, mix_norm, w_in, l, cos_full, sin_signed, q_norm, k_norm, seq)
        b_s_b = jnp.broadcast_to(a_b_s[l][:, :, None], (N_A_GROUPS, CHUNK, HEAD_DIM))
        a_n = _gating(uv, a_ln_g[l], a_ln_b[l], a_w_s[l], b_s_b, a_out_norm[l][None])
        os_, lses = [], []
        for gi, r in enumerate(DILATIONS):
            o, lse = _dilated_group(q_all, k_all, v_all, gi, r, batch, seq)
            os_.append(o)
            lses.append(lse)
        x2 = _out_proj(x2, a_n, os_, lses, b_out_norm, w_out, l)
        x2 = _ffn(x2, ffn_norm, w_gate, w_up, w_down, l)
    return x2.reshape(batch, seq, d_model)
```

```python
import functools
import math

import jax
import jax.numpy as jnp
from jax import lax
from jax.experimental import pallas as pl
from jax.experimental.pallas import tpu as pltpu

D_MODEL = 2048
HEAD_DIM = 128
CHUNK = 128
QB = 128
N_A_GROUPS = 8
A_WIDTH = N_A_GROUPS * HEAD_DIM
N_B_HEADS = 8
DILATED_CONFIGS = ((128, 1), (512, 4), (2048, 16))
DILATIONS = tuple(r for _, r in DILATED_CONFIGS)
N_B_GROUPS = len(DILATED_CONFIGS)
B_WIDTH = N_B_HEADS * HEAD_DIM
MIX_WIDTH = A_WIDTH + B_WIDTH
IN_WIDTH = 2 * A_WIDTH + N_B_GROUPS * B_WIDTH + 2 * B_WIDTH
D_FF = -(-8 * D_MODEL // (3 * 256)) * 256
ROPE_THETA = 10000.0
EPS = 1e-6

IN_TILE_N = 1024
IN_N_BLOCKS = IN_WIDTH // IN_TILE_N
IN_GELU_BLOCKS = 2 * A_WIDTH // IN_TILE_N
IN_Q_BLOCK0 = IN_GELU_BLOCKS
IN_K_BLOCK = IN_Q_BLOCK0 + N_B_GROUPS
IN_V_BLOCK = IN_K_BLOCK + 1
IN_GROUP = 4
IN_GROUP_STEPS = IN_GROUP * IN_N_BLOCKS
SUB_N = 256
HEADS_PER_SUB = SUB_N // HEAD_DIM

TILE_M = 512
EPI_ROWS = 64
EPI_SLOTS = 4
GATE_ROWS = 2048
FFN_TILE_M = 1024
FFN_TILE_F = 512
FFN_VMEM_LIMIT = (4 * FFN_TILE_M * D_MODEL * 4 + FFN_TILE_M * D_MODEL * 2
                  + 6 * D_MODEL * FFN_TILE_F * 2 + FFN_TILE_M * FFN_TILE_F * 14)
MAX_ATT_ROWS = 8 * QB
OUT_SUB_N = 256
OUT_SLOTS = 2
VMEM_LIMIT = 56 * 1024 * 1024

F32 = jnp.float32
BF16 = jnp.bfloat16
SQRT_HALF = float(math.sqrt(0.5))
LOG2_E = float(math.log2(math.e))
LN_2 = float(math.log(2.0))
NEG_INF = float("-inf")


def _rms_normalize(x, gain):
    ms = jnp.mean(x * x, axis=-1, keepdims=True)
    return x * lax.rsqrt(ms + EPS) * gain


def _gelu_exact(x):
    return 0.5 * x * (1.0 + lax.erf(x * SQRT_HALF))


def _head(h):
    return slice(h * HEAD_DIM, (h + 1) * HEAD_DIM)


def _in_proj_kernel(x_ref, g_ref, w_ref, cos_ref, sin_ref, qg_ref, kg_ref,
                    uv_ref, q_ref, k_ref, v_ref, h_ref, acc_ref, s_ref, t_ref,
                    inv_ref, y_ref, r_ref, hn_ref, *, pos_blocks):
    t = pl.program_id(0)
    in_group = t % IN_GROUP_STEPS
    half = t % IN_GROUP
    e = jnp.maximum(t - 1, 0)
    epi_tile = IN_GROUP * (e // IN_GROUP_STEPS) + e % IN_GROUP
    pos0 = (epi_tile % pos_blocks) * TILE_M

    def rope_rows(ref, rows):
        return ref[pl.ds(pl.multiple_of(pos0 + rows.start, EPI_ROWS), EPI_ROWS), :]

    def normalize_rows():
        h_ref[half] = _rms_normalize(x_ref[...], g_ref[...]).astype(BF16)

    def normalize_part(j, parts):
        rows = slice(j * (TILE_M // parts), (j + 1) * (TILE_M // parts))
        hn_ref[rows, :] = _rms_normalize(x_ref[rows, :], g_ref[...]).astype(BF16)

    def multiply(j):
        res = jnp.dot(h_ref[half], w_ref[:, j * SUB_N:(j + 1) * SUB_N],
                      preferred_element_type=F32)
        for hh in range(HEADS_PER_SUB):
            acc_ref[j * HEADS_PER_SUB + hh] = res[:, _head(hh)]

    def chunks():
        return [slice(c, c + EPI_ROWS) for c in range(0, TILE_M, EPI_ROWS)]

    def gelu_epilogue(h):
        for rows in chunks():
            uv_ref[rows, _head(h)] = _gelu_exact(acc_ref[h, rows, :]).astype(BF16)

    def split4(src, h, dst_f32, dst_bf16):
        rows = TILE_M // 4
        for c in range(4):
            piece = src[h, pl.ds(c, rows, stride=4), :]
            if dst_f32 is not None:
                dst_f32[h, c * rows:(c + 1) * rows, :] = piece
            if dst_bf16 is not None:
                dst_bf16[c * rows:(c + 1) * rows, _head(h)] = piece.astype(BF16)

    def split16_from4(h, dst_bf16):
        rows4 = TILE_M // 4
        rows16 = TILE_M // 16
        for c_lo in range(4):
            for c_hi in range(4):
                c = 4 * c_hi + c_lo
                piece = t_ref[h, pl.ds(c_lo * rows4 + c_hi, rows16, stride=4), :]
                dst_bf16[c * rows16:(c + 1) * rows16, _head(h)] = piece.astype(BF16)

    def residue_layouts(src, h, dsts):
        need16 = 16 in dsts
        if 4 in dsts or need16:
            split4(src, h, t_ref if need16 else None, dsts.get(4))
        if need16:
            split16_from4(h, dsts[16])

    def qk_epilogue(gain, dsts, h):
        strided = any(r > 1 for r in dsts)
        w = h % EPI_SLOTS
        for rows in chunks():
            a = acc_ref[h, rows, :]
            ms = jnp.mean(a * a, axis=-1, keepdims=True)
            inv_ref[w, rows, :] = jnp.broadcast_to(lax.rsqrt(ms + EPS), (EPI_ROWS, HEAD_DIM))
        for rows in chunks():
            y_ref[w, rows, :] = acc_ref[h, rows, :] * inv_ref[w, rows, :] * gain
        for rows in chunks():
            r_ref[w, rows, :] = pltpu.roll(y_ref[w, rows, :], HEAD_DIM // 2, 1)
        for rows in chunks():
            y = (y_ref[w, rows, :] * rope_rows(cos_ref, rows)
                 + r_ref[w, rows, :] * rope_rows(sin_ref, rows))
            if 1 in dsts:
                dsts[1][rows, _head(h)] = y.astype(BF16)
            if strided:
                s_ref[h, rows, :] = y
        residue_layouts(s_ref, h, dsts)

    def v_epilogue(h):
        dsts = {r: v_ref.at[gi] for gi, r in enumerate(DILATIONS)}
        for rows in chunks():
            dsts[1][rows, _head(h)] = acc_ref[h, rows, :].astype(BF16)
        residue_layouts(acc_ref, h, dsts)

    q_gain = qg_ref[...] * (HEAD_DIM ** -0.5 * LOG2_E)
    epilogues = {}
    for b in range(IN_GELU_BLOCKS):
        epilogues[b] = gelu_epilogue
    for gi, r in enumerate(DILATIONS):
        epilogues[IN_Q_BLOCK0 + gi] = functools.partial(qk_epilogue, q_gain, {r: q_ref})
    epilogues[IN_K_BLOCK] = functools.partial(
        qk_epilogue, kg_ref[...], {r: k_ref.at[gi] for gi, r in enumerate(DILATIONS)})
    epilogues[IN_V_BLOCK] = v_epilogue

    pl.when(t < IN_GROUP)(normalize_rows)

    @pl.when(t == 0)
    def _():
        for j in range(IN_TILE_N // SUB_N):
            multiply(j)

    prev_block = ((t + IN_GROUP_STEPS - 1) % IN_GROUP_STEPS) // IN_GROUP
    last_multiply = in_group >= IN_GROUP_STEPS - IN_GROUP
    n_sub = IN_TILE_N // SUB_N
    for b in range(IN_N_BLOCKS):
        for renew in ((False, True) if b >= IN_K_BLOCK else (False,)):
            @pl.when((prev_block == b) & (t > 0) & (last_multiply == renew))
            def _(b=b, renew=renew):
                for j in range(n_sub):
                    for hh in range(HEADS_PER_SUB):
                        epilogues[b](j * HEADS_PER_SUB + hh)
                    multiply(j)
                    if renew:
                        normalize_part(j, n_sub)
                if renew:
                    h_ref[half] = hn_ref[...]


def _in_proj(x2, gain, w_all, layer, cos_full, sin_signed, q_gain, k_gain, seq):
    m = x2.shape[0]
    row_tiles = m // TILE_M
    g = IN_GROUP
    assert row_tiles % g == 0
    pos_blocks = seq // TILE_M
    ps = IN_GROUP_STEPS
    steps = (row_tiles // g) * ps + 1

    def x_tile(t):
        nxt = g * (t // ps + 1) + jnp.clip(t % ps - (ps - g), 0, g - 1)
        return jnp.minimum(jnp.where(t < g, t, nxt), row_tiles - 1)

    def epi(t):
        e = jnp.maximum(t - 1, 0)
        return g * (e // ps), e % ps

    def uv_index(t):
        base, r = epi(t)
        r = jnp.minimum(r, g * IN_GELU_BLOCKS - 1)
        return base + r % g, r // g

    def q_index(t):
        base, r = epi(t)
        r = jnp.clip(r - g * IN_Q_BLOCK0, 0, g * N_B_GROUPS - 1)
        return r // g, base + r % g, 0

    def kv_index(blk):
        def index(t):
            base, r = epi(t)
            return 0, base + jnp.clip(r - g * blk, 0, g - 1), 0
        return index

    return pl.pallas_call(
        functools.partial(_in_proj_kernel, pos_blocks=pos_blocks),
        grid=(steps,),
        in_specs=[
            pl.BlockSpec((TILE_M, D_MODEL), lambda t: (x_tile(t), 0)),
            pl.BlockSpec((None, 1, D_MODEL), lambda t: (layer, 0, 0)),
            pl.BlockSpec((None, D_MODEL, IN_TILE_N), lambda t: (layer, 0, (t % ps) // g)),
            pl.BlockSpec((seq, HEAD_DIM), lambda t: (0, 0), pipeline_mode=pl.Buffered(1)),
            pl.BlockSpec((seq, HEAD_DIM), lambda t: (0, 0), pipeline_mode=pl.Buffered(1)),
            pl.BlockSpec((None, 1, HEAD_DIM), lambda t: (layer, 0, 0)),
            pl.BlockSpec((None, 1, HEAD_DIM), lambda t: (layer, 0, 0)),
        ],
        out_specs=[
            pl.BlockSpec((TILE_M, IN_TILE_N), uv_index),
            pl.BlockSpec((None, TILE_M, B_WIDTH), q_index),
            pl.BlockSpec((N_B_GROUPS, TILE_M, B_WIDTH), kv_index(IN_K_BLOCK)),
            pl.BlockSpec((N_B_GROUPS, TILE_M, B_WIDTH), kv_index(IN_V_BLOCK)),
        ],
        out_shape=[
            jax.ShapeDtypeStruct((m, 2 * A_WIDTH), BF16),
            jax.ShapeDtypeStruct((N_B_GROUPS, m, B_WIDTH), BF16),
            jax.ShapeDtypeStruct((N_B_GROUPS, m, B_WIDTH), BF16),
            jax.ShapeDtypeStruct((N_B_GROUPS, m, B_WIDTH), BF16),
        ],
        scratch_shapes=[
            pltpu.VMEM((IN_GROUP, TILE_M, D_MODEL), BF16),
            pltpu.VMEM((N_B_HEADS, TILE_M, HEAD_DIM), F32),
            pltpu.VMEM((N_B_HEADS, TILE_M, HEAD_DIM), F32),
            pltpu.VMEM((N_B_HEADS, TILE_M, HEAD_DIM), F32),
            pltpu.VMEM((EPI_SLOTS, TILE_M, HEAD_DIM), F32),
            pltpu.VMEM((EPI_SLOTS, TILE_M, HEAD_DIM), F32),
            pltpu.VMEM((EPI_SLOTS, TILE_M, HEAD_DIM), F32),
            pltpu.VMEM((TILE_M, D_MODEL), BF16),
        ],
        compiler_params=pltpu.CompilerParams(
            dimension_semantics=("arbitrary",),
            vmem_limit_bytes=VMEM_LIMIT),
        name="in_proj",
    )(x2, gain, w_all, cos_full, sin_signed, q_gain, k_gain)


def _gating_kernel(z_ref, lng_ref, lnb_ref, ws_ref, bs_ref, og_ref, o_ref,
                   wm_ref, vn_ref, a_ref, mu_ref, is_ref, mx_ref, rs_ref):
    row = lax.broadcasted_iota(jnp.int32, (CHUNK, CHUNK), 0)
    col = lax.broadcasted_iota(jnp.int32, (CHUNK, CHUNK), 1)
    for g in range(N_A_GROUPS):
        wm_ref[g] = jnp.where(row >= col, ws_ref[g], 0.0).astype(BF16)

    def chunk_rows(c):
        return slice(c * CHUNK, (c + 1) * CHUNK)

    def v_tile(c, g):
        return z_ref[chunk_rows(c), A_WIDTH + g * HEAD_DIM:A_WIDTH + (g + 1) * HEAD_DIM].astype(F32)

    tile = (CHUNK, HEAD_DIM)

    def ln_mean(c):
        for g in range(N_A_GROUPS):
            mu = jnp.mean(v_tile(c, g), axis=-1, keepdims=True)
            mu_ref[c % 2, g] = jnp.broadcast_to(mu, tile)

    def ln_var(c):
        for g in range(N_A_GROUPS):
            d = v_tile(c, g) - mu_ref[c % 2, g]
            var = jnp.mean(d * d, axis=-1, keepdims=True)
            is_ref[c % 2, g] = jnp.broadcast_to(lax.rsqrt(var + EPS), tile)

    def ln_apply(c):
        for g in range(N_A_GROUPS):
            vn = ((v_tile(c, g) - mu_ref[c % 2, g]) * is_ref[c % 2, g]
                  * lng_ref[g:g + 1, :] + lnb_ref[g:g + 1, :])
            vn_ref[chunk_rows(c), _head(g)] = vn.astype(BF16)

    def mix(c):
        for g in range(N_A_GROUPS):
            mx_ref[c % 2, g] = jnp.dot(wm_ref[g], vn_ref[chunk_rows(c), _head(g)],
                                       preferred_element_type=F32)

    def gate(c):
        sq = jnp.zeros(tile, F32)
        for g in range(N_A_GROUPS):
            a = z_ref[chunk_rows(c), _head(g)].astype(F32) * (mx_ref[c % 2, g] + bs_ref[g])
            sq = sq + a * a
            a_ref[chunk_rows(c), _head(g)] = a
        ms = jnp.sum(sq, axis=-1, keepdims=True) * (1.0 / A_WIDTH)
        rs_ref[c % 2] = jnp.broadcast_to(lax.rsqrt(ms + EPS), tile)

    def finish(c):
        for g in range(N_A_GROUPS):
            o_ref[chunk_rows(c), _head(g)] = (
                a_ref[chunk_rows(c), _head(g)] * rs_ref[c % 2] * og_ref[:, _head(g)]
            ).astype(BF16)

    stages = (ln_mean, ln_var, ln_apply, mix, gate, finish)
    n_chunks = GATE_ROWS // CHUNK
    for it in range(n_chunks + len(stages) - 1):
        for k in reversed(range(len(stages))):
            c = it - k
            if 0 <= c < n_chunks:
                stages[k](c)


def _gating(uv, ln_g, ln_b, w_s, b_s_b, out_gain):
    m = uv.shape[0]
    return pl.pallas_call(
        _gating_kernel,
        grid=(m // GATE_ROWS,),
        in_specs=[
            pl.BlockSpec((GATE_ROWS, 2 * A_WIDTH), lambda i: (i, 0)),
            pl.BlockSpec((N_A_GROUPS, HEAD_DIM), lambda i: (0, 0)),
            pl.BlockSpec((N_A_GROUPS, HEAD_DIM), lambda i: (0, 0)),
            pl.BlockSpec((N_A_GROUPS, CHUNK, CHUNK), lambda i: (0, 0, 0)),
            pl.BlockSpec((N_A_GROUPS, CHUNK, HEAD_DIM), lambda i: (0, 0, 0)),
            pl.BlockSpec((1, A_WIDTH), lambda i: (0, 0)),
        ],
        out_specs=pl.BlockSpec((GATE_ROWS, A_WIDTH), lambda i: (i, 0)),
        out_shape=jax.ShapeDtypeStruct((m, A_WIDTH), BF16),
        scratch_shapes=[
            pltpu.VMEM((N_A_GROUPS, CHUNK, CHUNK), BF16),
            pltpu.VMEM((GATE_ROWS, A_WIDTH), BF16),
            pltpu.VMEM((GATE_ROWS, A_WIDTH), F32),
            pltpu.VMEM((2, N_A_GROUPS, CHUNK, HEAD_DIM), F32),
            pltpu.VMEM((2, N_A_GROUPS, CHUNK, HEAD_DIM), F32),
            pltpu.VMEM((2, N_A_GROUPS, CHUNK, HEAD_DIM), F32),
            pltpu.VMEM((2, CHUNK, HEAD_DIM), F32),
        ],
        compiler_params=pltpu.CompilerParams(
            dimension_semantics=("parallel",), vmem_limit_bytes=VMEM_LIMIT),
        name="gating",
    )(uv, ln_g, ln_b, w_s, b_s_b, out_gain)


def _dilated_kernel(q_ref, k_ref, v_ref, o_ref, lse_ref, kk_ref, ve_ref,
                    sc_ref, m_ref, p_ref, *, piece_rows, att_rows):
    s = pl.program_id(2)
    n_pieces = att_rows // piece_rows
    win = QB + att_rows

    @pl.when(s == 0)
    def _():
        kk_ref[0:QB, :] = jnp.zeros((QB, B_WIDTH), BF16)
        ve_ref[:, 0:QB, 0:HEAD_DIM] = jnp.zeros((N_B_HEADS, QB, HEAD_DIM), BF16)
        ve_ref[:, :, HEAD_DIM:] = jnp.ones((N_B_HEADS, win, HEAD_DIM), BF16)

    @pl.when(s > 0)
    def _():
        kk_ref[0:QB, :] = kk_ref[att_rows:win, :]
        ve_ref[:, 0:QB, 0:HEAD_DIM] = ve_ref[:, att_rows:win, 0:HEAD_DIM]

    for p in range(n_pieces):
        rows = slice(QB + p * piece_rows, QB + (p + 1) * piece_rows)
        kk_ref[rows, :] = k_ref[p]
        for h in range(N_B_HEADS):
            ve_ref[h, rows, 0:HEAD_DIM] = v_ref[p, :, _head(h)]

    qi = lax.broadcasted_iota(jnp.int32, (QB, 2 * QB), 0)
    kj = lax.broadcasted_iota(jnp.int32, (QB, 2 * QB), 1)
    band = (kj >= qi) & (kj <= qi + QB)
    bias_band = jnp.where(band, 0.0, NEG_INF)
    bias_first = jnp.where(band & (kj >= QB), 0.0, NEG_INF)
    lane = lax.broadcasted_iota(jnp.int32, (QB, HEAD_DIM), 1)
    dn = (((1,), (1,)), ((), ()))

    def block_rows(ref, j, cols):
        if piece_rows >= QB:
            per = piece_rows // QB
            return ref[j // per, (j % per) * QB:(j % per + 1) * QB, cols]
        per = QB // piece_rows
        return jnp.concatenate([ref[j * per + t, :, cols] for t in range(per)], axis=0)

    def store_rows(ref, j, cols, val):
        if piece_rows >= QB:
            per = piece_rows // QB
            ref[j // per, (j % per) * QB:(j % per + 1) * QB, cols] = val
            return
        per = QB // piece_rows
        for t in range(per):
            ref[j * per + t, :, cols] = val[t * piece_rows:(t + 1) * piece_rows]

    units = [(j, h) for j in range(att_rows // QB) for h in range(N_B_HEADS)]

    def scores(u):
        j, h = units[u]
        q = block_rows(q_ref, j, _head(h))
        sc = lax.dot_general(q, kk_ref[j * QB:j * QB + 2 * QB, _head(h)], dn,
                             preferred_element_type=F32)
        bias = jnp.where(s == 0, bias_first, bias_band) if j == 0 else bias_band
        sc_ref[u] = sc + bias

    def row_max(u):
        m = jnp.max(sc_ref[u], axis=-1, keepdims=True)
        m_ref[u] = jnp.broadcast_to(m, (QB, HEAD_DIM))

    def probs(u):
        m = m_ref[u]
        for half in range(2):
            p_ref[u, :, _head(half)] = jnp.exp2(sc_ref[u, :, _head(half)] - m).astype(BF16)

    def weighted_values(u):
        j, h = units[u]
        oe = jnp.dot(p_ref[u], ve_ref[h, j * QB:j * QB + 2 * QB, :],
                     preferred_element_type=F32)
        den = oe[:, HEAD_DIM:]
        store_rows(o_ref, j, _head(h), (oe[:, :HEAD_DIM] / den).astype(BF16))
        m_ref[u] = m_ref[u] * LN_2 + jnp.log(den)

    stages = (scores, row_max, probs, weighted_values)
    for it in range(-(len(stages) - 1), len(units)):
        for depth, stage in enumerate(stages):
            u = it + len(stages) - 1 - depth
            if 0 <= u < len(units):
                stage(u)

    for j in range(att_rows // QB):
        tile = jnp.zeros((QB, HEAD_DIM), F32)
        for h in range(N_B_HEADS):
            tile = jnp.where(lane == h, m_ref[j * N_B_HEADS + h], tile)
        store_rows(lse_ref, j, slice(None), tile)


def _dilated_group(q_all, k_all, v_all, gi, dilation, batch, seq):
    r = dilation
    att_rows = min(MAX_ATT_ROWS, seq // r)
    units = (att_rows // QB) * N_B_HEADS
    piece_rows = min(TILE_M // r, att_rows)
    n_pieces = att_rows // piece_rows
    tiles = seq // (r * piece_rows)
    steps = tiles // n_pieces
    view = lambda a: a.reshape(N_B_GROUPS, batch, tiles, r, piece_rows, B_WIDTH)
    in_spec = pl.BlockSpec((None, None, n_pieces, None, piece_rows, B_WIDTH),
                           lambda b, c, s: (gi, b, s, c, 0, 0))
    o, lse = pl.pallas_call(
        functools.partial(_dilated_kernel, piece_rows=piece_rows, att_rows=att_rows),
        grid=(batch, r, steps),
        in_specs=[in_spec, in_spec, in_spec],
        out_specs=[
            pl.BlockSpec((None, n_pieces, None, piece_rows, B_WIDTH),
                         lambda b, c, s: (b, s, c, 0, 0)),
            pl.BlockSpec((None, n_pieces, None, piece_rows, HEAD_DIM),
                         lambda b, c, s: (b, s, c, 0, 0)),
        ],
        out_shape=[
            jax.ShapeDtypeStruct((batch, tiles, r, piece_rows, B_WIDTH), BF16),
            jax.ShapeDtypeStruct((batch, tiles, r, piece_rows, HEAD_DIM), F32),
        ],
        scratch_shapes=[
            pltpu.VMEM((QB + att_rows, B_WIDTH), BF16),
            pltpu.VMEM((N_B_HEADS, QB + att_rows, 2 * HEAD_DIM), BF16),
            pltpu.VMEM((units, QB, 2 * QB), F32),
            pltpu.VMEM((units, QB, HEAD_DIM), F32),
            pltpu.VMEM((units, QB, 2 * QB), BF16),
        ],
        compiler_params=pltpu.CompilerParams(
            dimension_semantics=("parallel", "parallel", "arbitrary"),
            vmem_limit_bytes=VMEM_LIMIT),
        name=f"dilated_r{r}",
    )(view(q_all), view(k_all), view(v_all))
    return (o.reshape(batch * seq, B_WIDTH), lse.reshape(batch * seq, HEAD_DIM))


def _out_proj_kernel(x_ref, a_ref, o0_ref, o1_ref, o2_ref, l0_ref, l1_ref, l2_ref,
                     bg_ref, w_ref, out_ref, mix_ref, b_ref, sq_ref, rs_ref, al_ref,
                     sp_ref, lslab_ref, oslab_ref):
    t = pl.program_id(0)
    o_refs = (o0_ref, o1_ref, o2_ref)
    l_refs = (l0_ref, l1_ref, l2_ref)
    n_sub = D_MODEL // OUT_SUB_N
    tile = (TILE_M, HEAD_DIM)

    def project(src, j):
        cols = slice(j * OUT_SUB_N, (j + 1) * OUT_SUB_N)
        out_ref[:, cols] = x_ref[:, cols] + jnp.dot(
            mix_ref[src], w_ref[:, cols], preferred_element_type=F32)

    def to_position_order(dst, src, r, cols):
        rows = TILE_M // r
        for c in range(r):
            dst[pl.ds(c, rows, stride=r), :] = src[c * rows:(c + 1) * rows, cols].astype(F32)

    def group_weights():
        lses = []
        for gi, r in enumerate(DILATIONS):
            if r == 1:
                lses.append(l_refs[gi][...])
            else:
                to_position_order(lslab_ref.at[gi], l_refs[gi], r, slice(None))
                lses.append(lslab_ref[gi])
        mx = jnp.maximum(jnp.maximum(lses[0], lses[1]), lses[2])
        es = [jnp.exp(l - mx) for l in lses]
        den = es[0] + es[1] + es[2]
        for gi, e in enumerate(es):
            al = e / den
            hi = al.astype(BF16)
            al_ref[gi, :, :HEAD_DIM] = hi
            al_ref[gi, :, HEAD_DIM:] = (al - hi.astype(F32)).astype(BF16)

    src_lane = lax.broadcasted_iota(jnp.int32, (2 * HEAD_DIM, 2 * HEAD_DIM), 0) % HEAD_DIM
    dst_head = lax.broadcasted_iota(jnp.int32, (2 * HEAD_DIM, 2 * HEAD_DIM), 1) // HEAD_DIM

    def spread_weights(h0):
        spread = jnp.where(src_lane == h0 + dst_head, 1.0, 0.0).astype(BF16)
        for gi in range(N_B_GROUPS):
            sp_ref[gi] = jnp.dot(al_ref[gi], spread, preferred_element_type=F32)

    def merge_head(h):
        b = None
        for gi, r in enumerate(DILATIONS):
            alpha = sp_ref[gi, :, _head(h % 2)]
            if r == 1:
                o = o_refs[gi][:, _head(h)].astype(F32)
            else:
                to_position_order(oslab_ref.at[gi, h % OUT_SLOTS], o_refs[gi], r, _head(h))
                o = oslab_ref[gi, h % OUT_SLOTS]
            b = alpha * o if b is None else b + alpha * o
        b_ref[:, _head(h)] = b
        sq_ref[...] = b * b if h == 0 else sq_ref[...] + b * b

    def inverse_rms():
        ms = jnp.sum(sq_ref[...], axis=-1, keepdims=True) * (1.0 / B_WIDTH)
        rs_ref[...] = jnp.broadcast_to(lax.rsqrt(ms + EPS), tile)

    def normalize(dst, h):
        cols = slice(A_WIDTH + h * HEAD_DIM, A_WIDTH + (h + 1) * HEAD_DIM)
        mix_ref[dst, :, cols] = (b_ref[:, _head(h)] * rs_ref[...] * bg_ref[:, _head(h)]).astype(BF16)

    def step(src, dst):
        subs = iter(range(n_sub))
        proj = (lambda: project(src, next(subs))) if src is not None else (lambda: None)
        proj()
        group_weights()
        mix_ref[dst, :, :A_WIDTH] = a_ref[...]
        proj()
        for h in range(N_B_HEADS):
            if h % 2 == 0:
                spread_weights(h)
            merge_head(h)
            if h % 2 == 1:
                proj()
        inverse_rms()
        for h in range(N_B_HEADS):
            normalize(dst, h)
            if h % 4 == 3:
                proj()

    pl.when(t == 0)(lambda: step(None, 0))
    for par in range(2):
        pl.when((t > 0) & (t % 2 == par))(functools.partial(step, 1 - par, par))


def _out_proj(x2, a_n, os_, lses, b_gain, w_all, layer):
    m = x2.shape[0]
    tiles = m // TILE_M
    done = lambda t: (jnp.maximum(t - 1, 0), 0)
    nxt = lambda t: (jnp.minimum(t, tiles - 1), 0)
    fixed = lambda t: (layer, 0, 0)
    return pl.pallas_call(
        _out_proj_kernel,
        grid=(tiles + 1,),
        in_specs=[
            pl.BlockSpec((TILE_M, D_MODEL), done),
            pl.BlockSpec((TILE_M, A_WIDTH), nxt),
            pl.BlockSpec((TILE_M, B_WIDTH), nxt),
            pl.BlockSpec((TILE_M, B_WIDTH), nxt),
            pl.BlockSpec((TILE_M, B_WIDTH), nxt),
            pl.BlockSpec((TILE_M, HEAD_DIM), nxt),
            pl.BlockSpec((TILE_M, HEAD_DIM), nxt),
            pl.BlockSpec((TILE_M, HEAD_DIM), nxt),
            pl.BlockSpec((None, 1, B_WIDTH), fixed),
            pl.BlockSpec((None, MIX_WIDTH, D_MODEL), fixed, pipeline_mode=pl.Buffered(1)),
        ],
        out_specs=pl.BlockSpec((TILE_M, D_MODEL), done),
        out_shape=jax.ShapeDtypeStruct((m, D_MODEL), F32),
        scratch_shapes=[
            pltpu.VMEM((2, TILE_M, MIX_WIDTH), BF16),
            pltpu.VMEM((TILE_M, B_WIDTH), F32),
            pltpu.VMEM((TILE_M, HEAD_DIM), F32),
            pltpu.VMEM((TILE_M, HEAD_DIM), F32),
            pltpu.VMEM((N_B_GROUPS, TILE_M, 2 * HEAD_DIM), BF16),
            pltpu.VMEM((N_B_GROUPS, TILE_M, 2 * HEAD_DIM), F32),
            pltpu.VMEM((N_B_GROUPS, TILE_M, HEAD_DIM), F32),
            pltpu.VMEM((N_B_GROUPS, OUT_SLOTS, TILE_M, HEAD_DIM), F32),
        ],
        compiler_params=pltpu.CompilerParams(
            dimension_semantics=("arbitrary",), vmem_limit_bytes=VMEM_LIMIT),
        name="out_proj",
    )(x2, a_n, *os_, *lses, b_gain, w_all)


def _ffn_kernel(x_ref, g_ref, wg_ref, wu_ref, wd_ref, out_ref, h_ref):
    f = pl.program_id(1)

    @pl.when(f == 0)
    def _():
        x = x_ref[...]
        h_ref[...] = _rms_normalize(x, g_ref[...]).astype(BF16)
        out_ref[...] = x

    h = h_ref[...]
    gate = jnp.dot(h, wg_ref[...], preferred_element_type=F32)
    up = jnp.dot(h, wu_ref[...], preferred_element_type=F32)
    act = (gate * jax.nn.sigmoid(gate) * up).astype(BF16)
    out_ref[...] += jnp.dot(act, wd_ref[...], preferred_element_type=F32)


def _ffn(x2, gain, w_gate, w_up, w_down, layer):
    m = x2.shape[0]
    return pl.pallas_call(
        _ffn_kernel,
        grid=(m // FFN_TILE_M, D_FF // FFN_TILE_F),
        in_specs=[
            pl.BlockSpec((FFN_TILE_M, D_MODEL), lambda i, f: (i, 0)),
            pl.BlockSpec((None, 1, D_MODEL), lambda i, f: (layer, 0, 0)),
            pl.BlockSpec((None, D_MODEL, FFN_TILE_F), lambda i, f: (layer, 0, f)),
            pl.BlockSpec((None, D_MODEL, FFN_TILE_F), lambda i, f: (layer, 0, f)),
            pl.BlockSpec((None, FFN_TILE_F, D_MODEL), lambda i, f: (layer, f, 0)),
        ],
        out_specs=pl.BlockSpec((FFN_TILE_M, D_MODEL), lambda i, f: (i, 0)),
        out_shape=jax.ShapeDtypeStruct((m, D_MODEL), F32),
        scratch_shapes=[pltpu.VMEM((FFN_TILE_M, D_MODEL), BF16)],
        compiler_params=pltpu.CompilerParams(
            dimension_semantics=("parallel", "arbitrary"),
            vmem_limit_bytes=FFN_VMEM_LIMIT),
        name="ffn",
    )(x2, gain, w_gate, w_up, w_down)


def _rope_tables(seq):
    pos = jnp.arange(seq, dtype=F32)
    inv_freq = 1.0 / (ROPE_THETA ** (jnp.arange(0, HEAD_DIM, 2, dtype=F32) / HEAD_DIM))
    ang = pos[:, None] * inv_freq[None, :]
    cos, sin = jnp.cos(ang), jnp.sin(ang)
    return (jnp.concatenate([cos, cos], axis=-1),
            jnp.concatenate([-sin, sin], axis=-1))


def kernel(x, mix_norm, w_in, a_ln_g, a_ln_b, a_w_s, a_b_s, q_norm, k_norm,
           a_out_norm, b_out_norm, w_out, ffn_norm, w_gate, w_up, w_down):
    batch, seq, d_model = x.shape
    depth = w_in.shape[0]
    assert d_model == D_MODEL and seq % TILE_M == 0
    assert all(w == r * QB for w, r in DILATED_CONFIGS)
    assert DILATIONS == (1, 4, 16)
    assert all(TILE_M % (r * 16) == 0 and seq % (r * 2 * QB) == 0 for r in DILATIONS)
    cos_full, sin_signed = _rope_tables(seq)
    x2 = x.reshape(batch * seq, d_model)
    w_in, w_out, w_gate, w_up, w_down = (
        w.astype(BF16) for w in (w_in, w_out, w_gate, w_up, w_down))
    as_rows = lambda p: p.reshape(depth, 1, p.shape[-1])
    mix_norm, q_norm, k_norm, b_out_norm, ffn_norm = (
        as_rows(p) for p in (mix_norm, q_norm, k_norm, b_out_norm, ffn_norm))
    for l in range(depth):
        uv, q_all, k_all, v_all = _in_proj(
            x2, mix_norm, w_in, l, cos_full, sin_signed, q_norm, k_norm, seq)
        b_s_b = jnp.broadcast_to(a_b_s[l][:, :, None], (N_A_GROUPS, CHUNK, HEAD_DIM))
        a_n = _gating(uv, a_ln_g[l], a_ln_b[l], a_w_s[l], b_s_b, a_out_norm[l][None])
        os_, lses = [], []
        for gi, r in enumerate(DILATIONS):
            o, lse = _dilated_group(q_all, k_all, v_all, gi, r, batch, seq)
            os_.append(o)
            lses.append(lse)
        x2 = _out_proj(x2, a_n, os_, lses, b_out_norm, w_out, l)
        x2 = _ffn(x2, ffn_norm, w_gate, w_up, w_down, l)
    return x2.reshape(batch, seq, d_model)
```

```python
import functools
import math

import jax
import jax.numpy as jnp
from jax import lax
from jax.experimental import pallas as pl
from jax.experimental.pallas import tpu as pltpu

D_MODEL = 2048
HEAD_DIM = 128
CHUNK = 128
QB = 128
N_A_GROUPS = 8
A_WIDTH = N_A_GROUPS * HEAD_DIM
N_B_HEADS = 8
DILATED_CONFIGS = ((128, 1), (512, 4), (2048, 16))
DILATIONS = tuple(r for _, r in DILATED_CONFIGS)
N_B_GROUPS = len(DILATED_CONFIGS)
B_WIDTH = N_B_HEADS * HEAD_DIM
MIX_WIDTH = A_WIDTH + B_WIDTH
IN_WIDTH = 2 * A_WIDTH + N_B_GROUPS * B_WIDTH + 2 * B_WIDTH
D_FF = -(-8 * D_MODEL // (3 * 256)) * 256
ROPE_THETA = 10000.0
EPS = 1e-6

IN_TILE_N = 1024
IN_N_BLOCKS = IN_WIDTH // IN_TILE_N
IN_GELU_BLOCKS = 2 * A_WIDTH // IN_TILE_N
IN_Q_BLOCK0 = IN_GELU_BLOCKS
IN_K_BLOCK = IN_Q_BLOCK0 + N_B_GROUPS
IN_V_BLOCK = IN_K_BLOCK + 1
IN_GROUP = 4
IN_GROUP_STEPS = IN_GROUP * IN_N_BLOCKS
SUB_N = 256
HEADS_PER_SUB = SUB_N // HEAD_DIM

TILE_M = 512
EPI_ROWS = 64
EPI_SLOTS = 4
GATE_ROWS = 2048
FFN_TILE_M = 1024
FFN_TILE_F = 512
FFN_VMEM_LIMIT = (4 * FFN_TILE_M * D_MODEL * 4 + FFN_TILE_M * D_MODEL * 2
                  + 6 * D_MODEL * FFN_TILE_F * 2 + FFN_TILE_M * FFN_TILE_F * 14)
MAX_ATT_ROWS = 8 * QB
OUT_SUB_N = 256
OUT_SLOTS = 2
VMEM_LIMIT = 56 * 1024 * 1024

F32 = jnp.float32
BF16 = jnp.bfloat16
SQRT_HALF = float(math.sqrt(0.5))
LOG2_E = float(math.log2(math.e))
LN_2 = float(math.log(2.0))
NEG_INF = float("-inf")


def _rms_normalize(x, gain):
    ms = jnp.mean(x * x, axis=-1, keepdims=True)
    return x * lax.rsqrt(ms + EPS) * gain


def _gelu_exact(x):
    return 0.5 * x * (1.0 + lax.erf(x * SQRT_HALF))


def _head(h):
    return slice(h * HEAD_DIM, (h + 1) * HEAD_DIM)


def _in_proj_kernel(x_ref, g_ref, w_ref, cos_ref, sin_ref, qg_ref, kg_ref,
                    uv_ref, q_ref, k_ref, v_ref, h_ref, acc_ref, s_ref, t_ref,
                    inv_ref, y_ref, r_ref, hn_ref, *, pos_blocks):
    t = pl.program_id(0)
    in_group = t % IN_GROUP_STEPS
    half = t % IN_GROUP
    e = jnp.maximum(t - 1, 0)
    epi_tile = IN_GROUP * (e // IN_GROUP_STEPS) + e % IN_GROUP
    pos0 = (epi_tile % pos_blocks) * TILE_M

    def rope_rows(ref, rows):
        return ref[pl.ds(pl.multiple_of(pos0 + rows.start, EPI_ROWS), EPI_ROWS), :]

    def normalize_rows():
        h_ref[half] = _rms_normalize(x_ref[...], g_ref[...]).astype(BF16)

    def normalize_part(j, parts):
        rows = slice(j * (TILE_M // parts), (j + 1) * (TILE_M // parts))
        hn_ref[rows, :] = _rms_normalize(x_ref[rows, :], g_ref[...]).astype(BF16)

    def multiply(j):
        res = jnp.dot(h_ref[half], w_ref[:, j * SUB_N:(j + 1) * SUB_N],
                      preferred_element_type=F32)
        for hh in range(HEADS_PER_SUB):
            acc_ref[j * HEADS_PER_SUB + hh] = res[:, _head(hh)]

    def chunks():
        return [slice(c, c + EPI_ROWS) for c in range(0, TILE_M, EPI_ROWS)]

    def gelu_epilogue(h):
        for rows in chunks():
            uv_ref[rows, _head(h)] = _gelu_exact(acc_ref[h, rows, :]).astype(BF16)

    def split4(src, h, dst_f32, dst_bf16):
        rows = TILE_M // 4
        for c in range(4):
            piece = src[h, pl.ds(c, rows, stride=4), :]
            if dst_f32 is not None:
                dst_f32[h, c * rows:(c + 1) * rows, :] = piece
            if dst_bf16 is not None:
                dst_bf16[c * rows:(c + 1) * rows, _head(h)] = piece.astype(BF16)

    def split16_from4(h, dst_bf16):
        rows4 = TILE_M // 4
        rows16 = TILE_M // 16
        for c_lo in range(4):
            for c_hi in range(4):
                c = 4 * c_hi + c_lo
                piece = t_ref[h, pl.ds(c_lo * rows4 + c_hi, rows16, stride=4), :]
                dst_bf16[c * rows16:(c + 1) * rows16, _head(h)] = piece.astype(BF16)

    def residue_layouts(src, h, dsts):
        need16 = 16 in dsts
        if 4 in dsts or need16:
            split4(src, h, t_ref if need16 else None, dsts.get(4))
        if need16:
            split16_from4(h, dsts[16])

    def qk_epilogue(gain, dsts, h):
        strided = any(r > 1 for r in dsts)
        w = h % EPI_SLOTS
        for rows in chunks():
            a = acc_ref[h, rows, :]
            ms = jnp.mean(a * a, axis=-1, keepdims=True)
            inv_ref[w, rows, :] = jnp.broadcast_to(lax.rsqrt(ms + EPS), (EPI_ROWS, HEAD_DIM))
        for rows in chunks():
            y_ref[w, rows, :] = acc_ref[h, rows, :] * inv_ref[w, rows, :] * gain
        for rows in chunks():
            r_ref[w, rows, :] = pltpu.roll(y_ref[w, rows, :], HEAD_DIM // 2, 1)
        for rows in chunks():
            y = (y_ref[w, rows, :] * rope_rows(cos_ref, rows)
                 + r_ref[w, rows, :] * rope_rows(sin_ref, rows))
            if 1 in dsts:
                dsts[1][rows, _head(h)] = y.astype(BF16)
            if strided:
                s_ref[h, rows, :] = y
        residue_layouts(s_ref, h, dsts)

    def v_epilogue(h):
        dsts = {r: v_ref.at[gi] for gi, r in enumerate(DILATIONS)}
        for rows in chunks():
            dsts[1][rows, _head(h)] = acc_ref[h, rows, :].astype(BF16)
        residue_layouts(acc_ref, h, dsts)

    q_gain = qg_ref[...] * (HEAD_DIM ** -0.5 * LOG2_E)
    epilogues = {}
    for b in range(IN_GELU_BLOCKS):
        epilogues[b] = gelu_epilogue
    for gi, r in enumerate(DILATIONS):
        epilogues[IN_Q_BLOCK0 + gi] = functools.partial(qk_epilogue, q_gain, {r: q_ref})
    epilogues[IN_K_BLOCK] = functools.partial(
        qk_epilogue, kg_ref[...], {r: k_ref.at[gi] for gi, r in enumerate(DILATIONS)})
    epilogues[IN_V_BLOCK] = v_epilogue

    pl.when(t < IN_GROUP)(normalize_rows)

    @pl.when(t == 0)
    def _():
        for j in range(IN_TILE_N // SUB_N):
            multiply(j)

    prev_block = ((t + IN_GROUP_STEPS - 1) % IN_GROUP_STEPS) // IN_GROUP
    last_multiply = in_group >= IN_GROUP_STEPS - IN_GROUP
    n_sub = IN_TILE_N // SUB_N
    for b in range(IN_N_BLOCKS):
        for renew in ((False, True) if b >= IN_K_BLOCK else (False,)):
            @pl.when((prev_block == b) & (t > 0) & (last_multiply == renew))
            def _(b=b, renew=renew):
                for j in range(n_sub):
                    for hh in range(HEADS_PER_SUB):
                        epilogues[b](j * HEADS_PER_SUB + hh)
                    multiply(j)
                    if renew:
                        normalize_part(j, n_sub)
                if renew:
                    h_ref[half] = hn_ref[...]


def _in_proj(x2, gain, w_all, layer, cos_full, sin_signed, q_gain, k_gain, seq):
    m = x2.shape[0]
    row_tiles = m // TILE_M
    g = IN_GROUP
    assert row_tiles % g == 0
    pos_blocks = seq // TILE_M
    ps = IN_GROUP_STEPS
    steps = (row_tiles // g) * ps + 1

    def x_tile(t):
        nxt = g * (t // ps + 1) + jnp.clip(t % ps - (ps - g), 0, g - 1)
        return jnp.minimum(jnp.where(t < g, t, nxt), row_tiles - 1)

    def epi(t):
        e = jnp.maximum(t - 1, 0)
        return g * (e // ps), e % ps

    def uv_index(t):
        base, r = epi(t)
        r = jnp.minimum(r, g * IN_GELU_BLOCKS - 1)
        return base + r % g, r // g

    def q_index(t):
        base, r = epi(t)
        r = jnp.clip(r - g * IN_Q_BLOCK0, 0, g * N_B_GROUPS - 1)
        return r // g, base + r % g, 0

    def kv_index(blk):
        def index(t):
            base, r = epi(t)
            return 0, base + jnp.clip(r - g * blk, 0, g - 1), 0
        return index

    return pl.pallas_call(
        functools.partial(_in_proj_kernel, pos_blocks=pos_blocks),
        grid=(steps,),
        in_specs=[
            pl.BlockSpec((TILE_M, D_MODEL), lambda t: (x_tile(t), 0)),
            pl.BlockSpec((None, 1, D_MODEL), lambda t: (layer, 0, 0)),
            pl.BlockSpec((None, D_MODEL, IN_TILE_N), lambda t: (layer, 0, (t % ps) // g)),
            pl.BlockSpec((seq, HEAD_DIM), lambda t: (0, 0), pipeline_mode=pl.Buffered(1)),
            pl.BlockSpec((seq, HEAD_DIM), lambda t: (0, 0), pipeline_mode=pl.Buffered(1)),
            pl.BlockSpec((None, 1, HEAD_DIM), lambda t: (layer, 0, 0)),
            pl.BlockSpec((None, 1, HEAD_DIM), lambda t: (layer, 0, 0)),
        ],
        out_specs=[
            pl.BlockSpec((TILE_M, IN_TILE_N), uv_index),
            pl.BlockSpec((None, TILE_M, B_WIDTH), q_index),
            pl.BlockSpec((N_B_GROUPS, TILE_M, B_WIDTH), kv_index(IN_K_BLOCK)),
            pl.BlockSpec((N_B_GROUPS, TILE_M, B_WIDTH), kv_index(IN_V_BLOCK)),
        ],
        out_shape=[
            jax.ShapeDtypeStruct((m, 2 * A_WIDTH), BF16),
            jax.ShapeDtypeStruct((N_B_GROUPS, m, B_WIDTH), BF16),
            jax.ShapeDtypeStruct((N_B_GROUPS, m, B_WIDTH), BF16),
            jax.ShapeDtypeStruct((N_B_GROUPS, m, B_WIDTH), BF16),
        ],
        scratch_shapes=[
            pltpu.VMEM((IN_GROUP, TILE_M, D_MODEL), BF16),
            pltpu.VMEM((N_B_HEADS, TILE_M, HEAD_DIM), F32),
            pltpu.VMEM((N_B_HEADS, TILE_M, HEAD_DIM), F32),
            pltpu.VMEM((N_B_HEADS, TILE_M, HEAD_DIM), F32),
            pltpu.VMEM((EPI_SLOTS, TILE_M, HEAD_DIM), F32),
            pltpu.VMEM((EPI_SLOTS, TILE_M, HEAD_DIM), F32),
            pltpu.VMEM((EPI_SLOTS, TILE_M, HEAD_DIM), F32),
            pltpu.VMEM((TILE_M, D_MODEL), BF16),
        ],
        compiler_params=pltpu.CompilerParams(
            dimension_semantics=("arbitrary",),
            vmem_limit_bytes=VMEM_LIMIT),
        name="in_proj",
    )(x2, gain, w_all, cos_full, sin_signed, q_gain, k_gain)


def _gating_kernel(z_ref, lng_ref, lnb_ref, ws_ref, bs_ref, og_ref, o_ref,
                   wm_ref, vn_ref, a_ref, mu_ref, is_ref, mx_ref, rs_ref):
    row = lax.broadcasted_iota(jnp.int32, (CHUNK, CHUNK), 0)
    col = lax.broadcasted_iota(jnp.int32, (CHUNK, CHUNK), 1)
    for g in range(N_A_GROUPS):
        wm_ref[g] = jnp.where(row >= col, ws_ref[g], 0.0).astype(BF16)

    def chunk_rows(c):
        return slice(c * CHUNK, (c + 1) * CHUNK)

    def v_tile(c, g):
        return z_ref[chunk_rows(c), A_WIDTH + g * HEAD_DIM:A_WIDTH + (g + 1) * HEAD_DIM].astype(F32)

    tile = (CHUNK, HEAD_DIM)

    def ln_mean(c):
        for g in range(N_A_GROUPS):
            mu = jnp.mean(v_tile(c, g), axis=-1, keepdims=True)
            mu_ref[c % 2, g] = jnp.broadcast_to(mu, tile)

    def ln_var(c):
        for g in range(N_A_GROUPS):
            d = v_tile(c, g) - mu_ref[c % 2, g]
            var = jnp.mean(d * d, axis=-1, keepdims=True)
            is_ref[c % 2, g] = jnp.broadcast_to(lax.rsqrt(var + EPS), tile)

    def ln_apply(c):
        for g in range(N_A_GROUPS):
            vn = ((v_tile(c, g) - mu_ref[c % 2, g]) * is_ref[c % 2, g]
                  * lng_ref[g:g + 1, :] + lnb_ref[g:g + 1, :])
            vn_ref[chunk_rows(c), _head(g)] = vn.astype(BF16)

    def mix(c):
        for g in range(N_A_GROUPS):
            mx_ref[c % 2, g] = jnp.dot(wm_ref[g], vn_ref[chunk_rows(c), _head(g)],
                                       preferred_element_type=F32)

    def gate(c):
        sq = jnp.zeros(tile, F32)
        for g in range(N_A_GROUPS):
            a = z_ref[chunk_rows(c), _head(g)].astype(F32) * (mx_ref[c % 2, g] + bs_ref[g])
            sq = sq + a * a
            a_ref[chunk_rows(c), _head(g)] = a
        ms = jnp.sum(sq, axis=-1, keepdims=True) * (1.0 / A_WIDTH)
        rs_ref[c % 2] = jnp.broadcast_to(lax.rsqrt(ms + EPS), tile)

    def finish(c):
        for g in range(N_A_GROUPS):
            o_ref[chunk_rows(c), _head(g)] = (
                a_ref[chunk_rows(c), _head(g)] * rs_ref[c % 2] * og_ref[:, _head(g)]
            ).astype(BF16)

    stages = (ln_mean, ln_var, ln_apply, mix, gate, finish)
    n_chunks = GATE_ROWS // CHUNK
    for it in range(n_chunks + len(stages) - 1):
        for k in reversed(range(len(stages))):
            c = it - k
            if 0 <= c < n_chunks:
                stages[k](c)


def _gating(uv, ln_g, ln_b, w_s, b_s_b, out_gain):
    m = uv.shape[0]
    return pl.pallas_call(
        _gating_kernel,
        grid=(m // GATE_ROWS,),
        in_specs=[
            pl.BlockSpec((GATE_ROWS, 2 * A_WIDTH), lambda i: (i, 0)),
            pl.BlockSpec((N_A_GROUPS, HEAD_DIM), lambda i: (0, 0)),
            pl.BlockSpec((N_A_GROUPS, HEAD_DIM), lambda i: (0, 0)),
            pl.BlockSpec((N_A_GROUPS, CHUNK, CHUNK), lambda i: (0, 0, 0)),
            pl.BlockSpec((N_A_GROUPS, CHUNK, HEAD_DIM), lambda i: (0, 0, 0)),
            pl.BlockSpec((1, A_WIDTH), lambda i: (0, 0)),
        ],
        out_specs=pl.BlockSpec((GATE_ROWS, A_WIDTH), lambda i: (i, 0)),
        out_shape=jax.ShapeDtypeStruct((m, A_WIDTH), BF16),
        scratch_shapes=[
            pltpu.VMEM((N_A_GROUPS, CHUNK, CHUNK), BF16),
            pltpu.VMEM((GATE_ROWS, A_WIDTH), BF16),
            pltpu.VMEM((GATE_ROWS, A_WIDTH), F32),
            pltpu.VMEM((2, N_A_GROUPS, CHUNK, HEAD_DIM), F32),
            pltpu.VMEM((2, N_A_GROUPS, CHUNK, HEAD_DIM), F32),
            pltpu.VMEM((2, N_A_GROUPS, CHUNK, HEAD_DIM), F32),
            pltpu.VMEM((2, CHUNK, HEAD_DIM), F32),
        ],
        compiler_params=pltpu.CompilerParams(
            dimension_semantics=("parallel",), vmem_limit_bytes=VMEM_LIMIT),
        name="gating",
    )(uv, ln_g, ln_b, w_s, b_s_b, out_gain)


def _dilated_kernel(q_ref, k_ref, v_ref, o_ref, lse_ref, kk_ref, ve_ref,
                    sc_ref, m_ref, p_ref, *, piece_rows, att_rows):
    s = pl.program_id(2)
    n_pieces = att_rows // piece_rows
    win = QB + att_rows

    @pl.when(s == 0)
    def _():
        kk_ref[0:QB, :] = jnp.zeros((QB, B_WIDTH), BF16)
        ve_ref[:, 0:QB, 0:HEAD_DIM] = jnp.zeros((N_B_HEADS, QB, HEAD_DIM), BF16)
        ve_ref[:, :, HEAD_DIM:] = jnp.ones((N_B_HEADS, win, HEAD_DIM), BF16)

    @pl.when(s > 0)
    def _():
        kk_ref[0:QB, :] = kk_ref[att_rows:win, :]
        ve_ref[:, 0:QB, 0:HEAD_DIM] = ve_ref[:, att_rows:win, 0:HEAD_DIM]

    for p in range(n_pieces):
        rows = slice(QB + p * piece_rows, QB + (p + 1) * piece_rows)
        kk_ref[rows, :] = k_ref[p]
        for h in range(N_B_HEADS):
            ve_ref[h, rows, 0:HEAD_DIM] = v_ref[p, :, _head(h)]

    qi = lax.broadcasted_iota(jnp.int32, (QB, 2 * QB), 0)
    kj = lax.broadcasted_iota(jnp.int32, (QB, 2 * QB), 1)
    band = (kj >= qi) & (kj <= qi + QB)
    bias_band = jnp.where(band, 0.0, NEG_INF)
    bias_first = jnp.where(band & (kj >= QB), 0.0, NEG_INF)
    lane = lax.broadcasted_iota(jnp.int32, (QB, HEAD_DIM), 1)
    dn = (((1,), (1,)), ((), ()))

    def block_rows(ref, j, cols):
        if piece_rows >= QB:
            per = piece_rows // QB
            return ref[j // per, (j % per) * QB:(j % per + 1) * QB, cols]
        per = QB // piece_rows
        return jnp.concatenate([ref[j * per + t, :, cols] for t in range(per)], axis=0)

    def store_rows(ref, j, cols, val):
        if piece_rows >= QB:
            per = piece_rows // QB
            ref[j // per, (j % per) * QB:(j % per + 1) * QB, cols] = val
            return
        per = QB // piece_rows
        for t in range(per):
            ref[j * per + t, :, cols] = val[t * piece_rows:(t + 1) * piece_rows]

    units = [(j, h) for j in range(att_rows // QB) for h in range(N_B_HEADS)]

    def scores(u):
        j, h = units[u]
        q = block_rows(q_ref, j, _head(h))
        sc = lax.dot_general(q, kk_ref[j * QB:j * QB + 2 * QB, _head(h)], dn,
                             preferred_element_type=F32)
        bias = jnp.where(s == 0, bias_first, bias_band) if j == 0 else bias_band
        sc_ref[u] = sc + bias

    def row_max(u):
        m = jnp.max(sc_ref[u], axis=-1, keepdims=True)
        m_ref[u] = jnp.broadcast_to(m, (QB, HEAD_DIM))

    def probs(u):
        m = m_ref[u]
        for half in range(2):
            p_ref[u, :, _head(half)] = jnp.exp2(sc_ref[u, :, _head(half)] - m).astype(BF16)

    def weighted_values(u):
        j, h = units[u]
        oe = jnp.dot(p_ref[u], ve_ref[h, j * QB:j * QB + 2 * QB, :],
                     preferred_element_type=F32)
        den = oe[:, HEAD_DIM:]
        store_rows(o_ref, j, _head(h), (oe[:, :HEAD_DIM] / den).astype(BF16))
        m_ref[u] = m_ref[u] * LN_2 + jnp.log(den)

    stages = (scores, row_max, probs, weighted_values)
    for it in range(-(len(stages) - 1), len(units)):
        for depth, stage in enumerate(stages):
            u = it + len(stages) - 1 - depth
            if 0 <= u < len(units):
                stage(u)

    for j in range(att_rows // QB):
        tile = jnp.zeros((QB, HEAD_DIM), F32)
        for h in range(N_B_HEADS):
            tile = jnp.where(lane == h, m_ref[j * N_B_HEADS + h], tile)
        store_rows(lse_ref, j, slice(None), tile)


def _dilated_group(q_all, k_all, v_all, gi, dilation, batch, seq):
    r = dilation
    att_rows = min(MAX_ATT_ROWS, seq // r)
    units = (att_rows // QB) * N_B_HEADS
    piece_rows = min(TILE_M // r, att_rows)
    n_pieces = att_rows // piece_rows
    tiles = seq // (r * piece_rows)
    steps = tiles // n_pieces
    view = lambda a: a.reshape(N_B_GROUPS, batch, tiles, r, piece_rows, B_WIDTH)
    in_spec = pl.BlockSpec((None, None, n_pieces, None, piece_rows, B_WIDTH),
                           lambda b, c, s: (gi, b, s, c, 0, 0))
    o, lse = pl.pallas_call(
        functools.partial(_dilated_kernel, piece_rows=piece_rows, att_rows=att_rows),
        grid=(batch, r, steps),
        in_specs=[in_spec, in_spec, in_spec],
        out_specs=[
            pl.BlockSpec((None, n_pieces, None, piece_rows, B_WIDTH),
                         lambda b, c, s: (b, s, c, 0, 0)),
            pl.BlockSpec((None, n_pieces, None, piece_rows, HEAD_DIM),
                         lambda b, c, s: (b, s, c, 0, 0)),
        ],
        out_shape=[
            jax.ShapeDtypeStruct((batch, tiles, r, piece_rows, B_WIDTH), BF16),
            jax.ShapeDtypeStruct((batch, tiles, r, piece_rows, HEAD_DIM), F32),
        ],
        scratch_shapes=[
            pltpu.VMEM((QB + att_rows, B_WIDTH), BF16),
            pltpu.VMEM((N_B_HEADS, QB + att_rows, 2 * HEAD_DIM), BF16),
            pltpu.VMEM((units, QB, 2 * QB), F32),
            pltpu.VMEM((units, QB, HEAD_DIM), F32),
            pltpu.VMEM((units, QB, 2 * QB), BF16),
        ],
        compiler_params=pltpu.CompilerParams(
            dimension_semantics=("parallel", "parallel", "arbitrary"),
            vmem_limit_bytes=VMEM_LIMIT),
        name=f"dilated_r{r}",
    )(view(q_all), view(k_all), view(v_all))
    return (o.reshape(batch * seq, B_WIDTH), lse.reshape(batch * seq, HEAD_DIM))


def _out_proj_kernel(x_ref, a_ref, o0_ref, o1_ref, o2_ref, l0_ref, l1_ref, l2_ref,
                     bg_ref, w_ref, out_ref, mix_ref, b_ref, sq_ref, rs_ref, al_ref,
                     sp_ref, lslab_ref, oslab_ref, obuf_ref, osem):
    t = pl.program_id(0)
    o_refs = (o0_ref, o1_ref, o2_ref)
    l_refs = (l0_ref, l1_ref, l2_ref)
    n_sub = D_MODEL // OUT_SUB_N
    tile = (TILE_M, HEAD_DIM)

    def project(src, j):
        cols = slice(j * OUT_SUB_N, (j + 1) * OUT_SUB_N)
        obuf_ref[src, :, cols] = x_ref[:, cols] + jnp.dot(
            mix_ref[src], w_ref[:, cols], preferred_element_type=F32)

    def write_back(slot, tile):
        rows = pl.ds(pl.multiple_of(tile * TILE_M, TILE_M), TILE_M)
        return pltpu.make_async_copy(obuf_ref.at[slot], out_ref.at[rows, :], osem.at[slot])

    def to_position_order(dst, src, r, cols):
        rows = TILE_M // r
        for c in range(r):
            dst[pl.ds(c, rows, stride=r), :] = src[c * rows:(c + 1) * rows, cols].astype(F32)

    def group_weights():
        lses = []
        for gi, r in enumerate(DILATIONS):
            if r == 1:
                lses.append(l_refs[gi][...])
            else:
                to_position_order(lslab_ref.at[gi], l_refs[gi], r, slice(None))
                lses.append(lslab_ref[gi])
        mx = jnp.maximum(jnp.maximum(lses[0], lses[1]), lses[2])
        es = [jnp.exp(l - mx) for l in lses]
        den = es[0] + es[1] + es[2]
        for gi, e in enumerate(es):
            al = e / den
            hi = al.astype(BF16)
            al_ref[gi, :, :HEAD_DIM] = hi
            al_ref[gi, :, HEAD_DIM:] = (al - hi.astype(F32)).astype(BF16)

    src_lane = lax.broadcasted_iota(jnp.int32, (2 * HEAD_DIM, 2 * HEAD_DIM), 0) % HEAD_DIM
    dst_head = lax.broadcasted_iota(jnp.int32, (2 * HEAD_DIM, 2 * HEAD_DIM), 1) // HEAD_DIM

    def spread_weights(h0):
        spread = jnp.where(src_lane == h0 + dst_head, 1.0, 0.0).astype(BF16)
        for gi in range(N_B_GROUPS):
            sp_ref[gi] = jnp.dot(al_ref[gi], spread, preferred_element_type=F32)

    def merge_head(h):
        b = None
        for gi, r in enumerate(DILATIONS):
            alpha = sp_ref[gi, :, _head(h % 2)]
            if r == 1:
                o = o_refs[gi][:, _head(h)].astype(F32)
            else:
                to_position_order(oslab_ref.at[gi, h % OUT_SLOTS], o_refs[gi], r, _head(h))
                o = oslab_ref[gi, h % OUT_SLOTS]
            b = alpha * o if b is None else b + alpha * o
        b_ref[:, _head(h)] = b
        sq_ref[...] = b * b if h == 0 else sq_ref[...] + b * b

    def inverse_rms():
        ms = jnp.sum(sq_ref[...], axis=-1, keepdims=True) * (1.0 / B_WIDTH)
        rs_ref[...] = jnp.broadcast_to(lax.rsqrt(ms + EPS), tile)

    def normalize(dst, h):
        cols = slice(A_WIDTH + h * HEAD_DIM, A_WIDTH + (h + 1) * HEAD_DIM)
        mix_ref[dst, :, cols] = (b_ref[:, _head(h)] * rs_ref[...] * bg_ref[:, _head(h)]).astype(BF16)

    def step(src, dst):
        subs = iter(range(n_sub))
        proj = (lambda: project(src, next(subs))) if src is not None else (lambda: None)
        if src is not None:
            pl.when(t >= 3)(lambda: write_back(src, t - 3).wait())
        proj()
        group_weights()
        mix_ref[dst, :, :A_WIDTH] = a_ref[...]
        proj()
        for h in range(N_B_HEADS):
            if h % 2 == 0:
                spread_weights(h)
            merge_head(h)
            if h % 2 == 1:
                proj()
        inverse_rms()
        for h in range(N_B_HEADS):
            normalize(dst, h)
            if h % 4 == 3:
                proj()
        if src is not None:
            write_back(src, t - 1).start()

            @pl.when(t == pl.num_programs(0) - 1)
            def _():
                write_back(dst, t - 2).wait()
                write_back(src, t - 1).wait()

    pl.when(t == 0)(lambda: step(None, 0))
    for par in range(2):
        pl.when((t > 0) & (t % 2 == par))(functools.partial(step, 1 - par, par))


def _out_proj(x2, a_n, os_, lses, b_gain, w_all, layer):
    m = x2.shape[0]
    tiles = m // TILE_M
    done = lambda t: (jnp.maximum(t - 1, 0), 0)
    nxt = lambda t: (jnp.minimum(t, tiles - 1), 0)
    fixed = lambda t: (layer, 0, 0)
    return pl.pallas_call(
        _out_proj_kernel,
        grid=(tiles + 1,),
        in_specs=[
            pl.BlockSpec((TILE_M, D_MODEL), done),
            pl.BlockSpec((TILE_M, A_WIDTH), nxt),
            pl.BlockSpec((TILE_M, B_WIDTH), nxt),
            pl.BlockSpec((TILE_M, B_WIDTH), nxt),
            pl.BlockSpec((TILE_M, B_WIDTH), nxt),
            pl.BlockSpec((TILE_M, HEAD_DIM), nxt),
            pl.BlockSpec((TILE_M, HEAD_DIM), nxt),
            pl.BlockSpec((TILE_M, HEAD_DIM), nxt),
            pl.BlockSpec((None, 1, B_WIDTH), fixed),
            pl.BlockSpec((None, MIX_WIDTH, D_MODEL), fixed, pipeline_mode=pl.Buffered(1)),
        ],
        out_specs=pl.BlockSpec(memory_space=pl.ANY),
        out_shape=jax.ShapeDtypeStruct((m, D_MODEL), F32),
        scratch_shapes=[
            pltpu.VMEM((2, TILE_M, MIX_WIDTH), BF16),
            pltpu.VMEM((TILE_M, B_WIDTH), F32),
            pltpu.VMEM((TILE_M, HEAD_DIM), F32),
            pltpu.VMEM((TILE_M, HEAD_DIM), F32),
            pltpu.VMEM((N_B_GROUPS, TILE_M, 2 * HEAD_DIM), BF16),
            pltpu.VMEM((N_B_GROUPS, TILE_M, 2 * HEAD_DIM), F32),
            pltpu.VMEM((N_B_GROUPS, TILE_M, HEAD_DIM), F32),
            pltpu.VMEM((N_B_GROUPS, OUT_SLOTS, TILE_M, HEAD_DIM), F32),
            pltpu.VMEM((2, TILE_M, D_MODEL), F32),
            pltpu.SemaphoreType.DMA((2,)),
        ],
        compiler_params=pltpu.CompilerParams(
            dimension_semantics=("arbitrary",), vmem_limit_bytes=VMEM_LIMIT),
        name="out_proj",
    )(x2, a_n, *os_, *lses, b_gain, w_all)


def _ffn_kernel(x_ref, g_ref, wg_ref, wu_ref, wd_ref, out_ref, h_ref):
    f = pl.program_id(1)

    @pl.when(f == 0)
    def _():
        x = x_ref[...]
        h_ref[...] = _rms_normalize(x, g_ref[...]).astype(BF16)
        out_ref[...] = x

    h = h_ref[...]
    gate = jnp.dot(h, wg_ref[...], preferred_element_type=F32)
    up = jnp.dot(h, wu_ref[...], preferred_element_type=F32)
    act = (gate * jax.nn.sigmoid(gate) * up).astype(BF16)
    out_ref[...] += jnp.dot(act, wd_ref[...], preferred_element_type=F32)


def _ffn(x2, gain, w_gate, w_up, w_down, layer):
    m = x2.shape[0]
    return pl.pallas_call(
        _ffn_kernel,
        grid=(m // FFN_TILE_M, D_FF // FFN_TILE_F),
        in_specs=[
            pl.BlockSpec((FFN_TILE_M, D_MODEL), lambda i, f: (i, 0)),
            pl.BlockSpec((None, 1, D_MODEL), lambda i, f: (layer, 0, 0)),
            pl.BlockSpec((None, D_MODEL, FFN_TILE_F), lambda i, f: (layer, 0, f)),
            pl.BlockSpec((None, D_MODEL, FFN_TILE_F), lambda i, f: (layer, 0, f)),
            pl.BlockSpec((None, FFN_TILE_F, D_MODEL), lambda i, f: (layer, f, 0)),
        ],
        out_specs=pl.BlockSpec((FFN_TILE_M, D_MODEL), lambda i, f: (i, 0)),
        out_shape=jax.ShapeDtypeStruct((m, D_MODEL), F32),
        scratch_shapes=[pltpu.VMEM((FFN_TILE_M, D_MODEL), BF16)],
        compiler_params=pltpu.CompilerParams(
            dimension_semantics=("parallel", "arbitrary"),
            vmem_limit_bytes=FFN_VMEM_LIMIT),
        name="ffn",
    )(x2, gain, w_gate, w_up, w_down)


def _rope_tables(seq):
    pos = jnp.arange(seq, dtype=F32)
    inv_freq = 1.0 / (ROPE_THETA ** (jnp.arange(0, HEAD_DIM, 2, dtype=F32) / HEAD_DIM))
    ang = pos[:, None] * inv_freq[None, :]
    cos, sin = jnp.cos(ang), jnp.sin(ang)
    return (jnp.concatenate([cos, cos], axis=-1),
            jnp.concatenate([-sin, sin], axis=-1))


def kernel(x, mix_norm, w_in, a_ln_g, a_ln_b, a_w_s, a_b_s, q_norm, k_norm,
           a_out_norm, b_out_norm, w_out, ffn_norm, w_gate, w_up, w_down):
    batch, seq, d_model = x.shape
    depth = w_in.shape[0]
    assert d_model == D_MODEL and seq % TILE_M == 0
    assert all(w == r * QB for w, r in DILATED_CONFIGS)
    assert DILATIONS == (1, 4, 16)
    assert all(TILE_M % (r * 16) == 0 and seq % (r * 2 * QB) == 0 for r in DILATIONS)
    cos_full, sin_signed = _rope_tables(seq)
    x2 = x.reshape(batch * seq, d_model)
    w_in, w_out, w_gate, w_up, w_down = (
        w.astype(BF16) for w in (w_in, w_out, w_gate, w_up, w_down))
    as_rows = lambda p: p.reshape(depth, 1, p.shape[-1])
    mix_norm, q_norm, k_norm, b_out_norm, ffn_norm = (
        as_rows(p) for p in (mix_norm, q_norm, k_norm, b_out_norm, ffn_norm))
    for l in range(depth):
        uv, q_all, k_all, v_all = _in_proj(
            x2, mix_norm, w_in, l, cos_full, sin_signed, q_norm, k_norm, seq)
        b_s_b = jnp.broadcast_to(a_b_s[l][:, :, None], (N_A_GROUPS, CHUNK, HEAD_DIM))
        a_n = _gating(uv, a_ln_g[l], a_ln_b[l], a_w_s[l], b_s_b, a_out_norm[l][None])
        os_, lses = [], []
        for gi, r in enumerate(DILATIONS):
            o, lse = _dilated_group(q_all, k_all, v_all, gi, r, batch, seq)
            os_.append(o)
            lses.append(lse)
        x2 = _out_proj(x2, a_n, os_, lses, b_out_norm, w_out, l)
        x2 = _ffn(x2, ffn_norm, w_gate, w_up, w_down, l)
    return x2.reshape(batch, seq, d_model)
```
